```python
import jax, jax.numpy as jnp
from jax import lax
import numpy as np

D_MODEL = 1024
BATCH = 8
SEQ = 4096
DEPTH = 4

N_MIXERS = 3
D_FF = 2816
NORM_EPS = 1e-5

GLA_HEADS = 4
GLA_DK = D_MODEL // 2
GLA_DV = D_MODEL
GLA_HEAD_K = GLA_DK // GLA_HEADS
GLA_HEAD_V = GLA_DV // GLA_HEADS
GLA_GATE_RANK = 16
GLA_GATE_NORMALIZER = 16.0
GLA_CHUNK = 64
GLA_IN = 2 * GLA_DK + 2 * GLA_DV + GLA_GATE_RANK

SGU_D_FFN = 6 * D_MODEL
SGU_HALF = SGU_D_FFN // 2
SGU_GROUPS = 8
SGU_GROUP_DIM = SGU_HALF // SGU_GROUPS
SGU_CHUNK = 128

SWA_HEADS = 16
SWA_KV_HEADS = 2
SWA_HEAD_DIM = 64
SWA_GROUP = SWA_HEADS // SWA_KV_HEADS
SWA_WINDOW = 128
SWA_BLOCK = 128
SWA_QKV = (SWA_HEADS + 2 * SWA_KV_HEADS) * SWA_HEAD_DIM
ROPE_DIM = SWA_HEAD_DIM // 4
ROPE_THETA = 500000.0

kernel_name = 'hybrid_gla_sgu_swa_macaron'


def rms_norm(x, gain):
    xf = x.astype(jnp.float32)
    y = xf * lax.rsqrt(jnp.mean(xf * xf, axis=-1, keepdims=True) + NORM_EPS)
    return (y * gain.astype(jnp.float32)).astype(x.dtype)


def layer_norm(x, gain, bias):
    xf = x.astype(jnp.float32)
    mu = jnp.mean(xf, axis=-1, keepdims=True)
    xc = xf - mu
    var = jnp.mean(xc * xc, axis=-1, keepdims=True)
    y = xc * lax.rsqrt(var + NORM_EPS) * gain.astype(jnp.float32) + bias.astype(jnp.float32)
    return y.astype(x.dtype)


def swiglu_ffn(h, w_in, w_out):
    gate, up = jnp.split(h @ w_in, 2, axis=-1)
    return (jax.nn.silu(gate) * up) @ w_out


def gla_mixer(h, w_in, w_gk_up, b_gk, o_norm, w_out):
    B, S, _ = h.shape
    f32 = jnp.float32
    q, k, v, r, gk_low = jnp.split(h @ w_in, [GLA_DK, 2 * GLA_DK, 2 * GLA_DK + GLA_DV, 2 * GLA_DK + 2 * GLA_DV], axis=-1)
    log_a = jax.nn.log_sigmoid((gk_low @ w_gk_up + b_gk).astype(f32)) / GLA_GATE_NORMALIZER
    n = S // GLA_CHUNK

    def to_chunks(t, dh):
        return t.reshape(B, n, GLA_CHUNK, GLA_HEADS, dh).transpose(1, 0, 3, 2, 4).astype(f32)

    q = to_chunks(q, GLA_HEAD_K) * (GLA_HEAD_K ** -0.5)
    k = to_chunks(k, GLA_HEAD_K)
    v = to_chunks(v, GLA_HEAD_V)
    b = jnp.cumsum(to_chunks(log_a, GLA_HEAD_K), axis=3)
    b_last = b[:, :, :, -1:, :]
    q_dec = q * jnp.exp(b)
    k_intra = k * jnp.exp(-b)
    k_state = k * jnp.exp(b_last - b)
    causal = jnp.tril(jnp.ones((GLA_CHUNK, GLA_CHUNK), dtype=bool))
    attn = jnp.where(causal, jnp.einsum('nbhik,nbhjk->nbhij', q_dec, k_intra), 0.0)
    o_intra = jnp.einsum('nbhij,nbhjv->nbhiv', attn, v)

    def step(state, xs):
        q_c, k_c, v_c, decay_c = xs
        o_inter = jnp.einsum('bhik,bhkv->bhiv', q_c, state)
        state = state * decay_c[:, :, 0, :, None] + jnp.einsum('bhjk,bhjv->bhkv', k_c, v_c)
        return state, o_inter

    state0 = jnp.zeros((B, GLA_HEADS, GLA_HEAD_K, GLA_HEAD_V), f32)
    _, o_inter = lax.scan(step, state0, (q_dec, k_state, v, jnp.exp(b_last)))
    o = (o_intra + o_inter).transpose(1, 0, 3, 2, 4).reshape(B, S, GLA_HEADS, GLA_HEAD_V)
    o = rms_norm(o, o_norm).reshape(B, S, GLA_DV) * jax.nn.silu(r.astype(f32))
    return o.astype(h.dtype) @ w_out


def sgu_mixer(h, w_in, ln_gain, ln_bias, w_s, b_s, w_out):
    B, S, _ = h.shape
    u, v = jnp.split(jax.nn.gelu(h @ w_in, approximate=False), 2, axis=-1)
    v = layer_norm(v, ln_gain, ln_bias)
    n = S // SGU_CHUNK
    v = v.reshape(B, n, SGU_CHUNK, SGU_GROUPS, SGU_GROUP_DIM)
    causal = jnp.tril(jnp.ones((SGU_CHUNK, SGU_CHUNK), dtype=bool))
    w_causal = jnp.where(causal[None], w_s, 0.0)
    v = jnp.einsum('gij,bnjgd->bnigd', w_causal.astype(v.dtype), v) + b_s.T[None, None, :, :, None]
    return (u * v.reshape(B, S, SGU_HALF)) @ w_out


def rope_tables(positions):
    inv_freq = ROPE_THETA ** (-jnp.arange(0, ROPE_DIM, 2, dtype=jnp.float32) / ROPE_DIM)
    ang = positions.astype(jnp.float32)[..., None] * inv_freq
    return jnp.cos(ang)[:, :, None, :], jnp.sin(ang)[:, :, None, :]


def rope_partial(t, cos, sin):
    cos = cos.astype(t.dtype)
    sin = sin.astype(t.dtype)
    half = ROPE_DIM // 2
    x1, x2, rest = t[..., :half], t[..., half:ROPE_DIM], t[..., ROPE_DIM:]
    return jnp.concatenate([x1 * cos - x2 * sin, x2 * cos + x1 * sin, rest], axis=-1)


def swa_mixer(h, cos, sin, w_qkv, b_qkv, sinks, w_out, b_out):
    B, S, _ = h.shape
    HD = SWA_HEAD_DIM
    q, k, v = jnp.split(h @ w_qkv + b_qkv, [SWA_HEADS * HD, (SWA_HEADS + SWA_KV_HEADS) * HD], axis=-1)
    q = rope_partial(q.reshape(B, S, SWA_HEADS, HD), cos, sin)
    k = rope_partial(k.reshape(B, S, SWA_KV_HEADS, HD), cos, sin)
    v = v.reshape(B, S, SWA_KV_HEADS, HD)
    n = S // SWA_BLOCK
    qb = q.reshape(B, n, SWA_BLOCK, SWA_KV_HEADS, SWA_GROUP, HD)
    kb = k.reshape(B, n, SWA_BLOCK, SWA_KV_HEADS, HD)
    vb = v.reshape(B, n, SWA_BLOCK, SWA_KV_HEADS, HD)

    def band(t):
        prev = jnp.concatenate([jnp.zeros_like(t[:, :1]), t[:, :-1]], axis=1)
        return jnp.concatenate([prev, t], axis=2)

    k_band, v_band = band(kb), band(vb)
    scores = jnp.einsum('bnqkgd,bnskd->bnkgqs', qb, k_band).astype(jnp.float32) * (HD ** -0.5)
    qi = jnp.arange(SWA_BLOCK)[:, None] + SWA_BLOCK
    kj = jnp.arange(2 * SWA_BLOCK)[None, :]
    delta = qi - kj
    in_window = (delta >= 0) & (delta < SWA_WINDOW)
    not_pad = (jnp.arange(n)[:, None, None] > 0) | (kj >= SWA_BLOCK)[None]
    valid = in_window[None] & not_pad
    scores = jnp.where(valid[None, :, None, None], scores, -jnp.inf)
    sink = jnp.broadcast_to(sinks.astype(jnp.float32).reshape(1, 1, SWA_KV_HEADS, SWA_GROUP, 1, 1), scores.shape[:-1] + (1,))
    probs = jax.nn.softmax(jnp.concatenate([scores, sink], axis=-1), axis=-1)[..., :-1]
    out = jnp.einsum('bnkgqs,bnskd->bnqkgd', probs.astype(v.dtype), v_band)
    return out.reshape(B, S, SWA_HEADS * HD) @ w_out + b_out


def setup_inputs(seed: int = 0) -> dict:
    key = jax.random.key(seed)
    keys = iter(jax.random.split(key, 128))

    def normal(shape, scale):
        return scale * jax.random.normal(next(keys), shape, jnp.float32)

    def gain(dim):
        return 1.0 + normal((dim,), 0.02)

    inp = {}
    inp['x'] = normal((BATCH, SEQ, D_MODEL), 1.0)
    inp['positions'] = jnp.broadcast_to(jnp.arange(SEQ, dtype=jnp.int32), (BATCH, SEQ))

    def ffn(prefix):
        inp[prefix + '_norm'] = gain(D_MODEL)
        inp[prefix + '_w_in'] = normal((D_MODEL, 2 * D_FF), D_MODEL ** -0.5)
        inp[prefix + '_w_out'] = normal((D_FF, D_MODEL), D_FF ** -0.5)

    for layer in range(DEPTH):
        p = 'l%d' % layer
        ffn(p + '_ffn1')
        inp[p + '_mix_norm'] = gain(D_MODEL)
        kind = layer % N_MIXERS
        if kind == 0:
            inp[p + '_gla_w_in'] = normal((D_MODEL, GLA_IN), D_MODEL ** -0.5)
            inp[p + '_gla_w_gk_up'] = normal((GLA_GATE_RANK, GLA_DK), GLA_GATE_RANK ** -0.5)
            inp[p + '_gla_b_gk'] = normal((GLA_DK,), 0.02)
            inp[p + '_gla_o_norm'] = gain(GLA_HEAD_V)
            inp[p + '_gla_w_out'] = normal((GLA_DV, D_MODEL), GLA_DV ** -0.5)
        elif kind == 1:
            inp[p + '_sgu_w_in'] = normal((D_MODEL, SGU_D_FFN), D_MODEL ** -0.5)
            inp[p + '_sgu_ln_gain'] = gain(SGU_HALF)
            inp[p + '_sgu_ln_bias'] = normal((SGU_HALF,), 0.02)
            inp[p + '_sgu_w_s'] = normal((SGU_GROUPS, SGU_CHUNK, SGU_CHUNK), SGU_CHUNK ** -0.5)
            inp[p + '_sgu_b_s'] = 1.0 + normal((SGU_GROUPS, SGU_CHUNK), 0.02)
            inp[p + '_sgu_w_out'] = normal((SGU_HALF, D_MODEL), SGU_HALF ** -0.5)
        else:
            inp[p + '_swa_w_qkv'] = normal((D_MODEL, SWA_QKV), D_MODEL ** -0.5)
            inp[p + '_swa_b_qkv'] = normal((SWA_QKV,), 0.02)
            inp[p + '_swa_sinks'] = normal((SWA_HEADS,), 0.5)
            inp[p + '_swa_w_out'] = normal((SWA_HEADS * SWA_HEAD_DIM, D_MODEL), (SWA_HEADS * SWA_HEAD_DIM) ** -0.5)
            inp[p + '_swa_b_out'] = normal((D_MODEL,), 0.02)
        ffn(p + '_ffn2')
    inp['final_norm'] = gain(D_MODEL)
    return inp


def reference(x, positions,
              l0_ffn1_norm, l0_ffn1_w_in, l0_ffn1_w_out, l0_mix_norm,
              l0_gla_w_in, l0_gla_w_gk_up, l0_gla_b_gk, l0_gla_o_norm, l0_gla_w_out,
              l0_ffn2_norm, l0_ffn2_w_in, l0_ffn2_w_out,
              l1_ffn1_norm, l1_ffn1_w_in, l1_ffn1_w_out, l1_mix_norm,
              l1_sgu_w_in, l1_sgu_ln_gain, l1_sgu_ln_bias, l1_sgu_w_s, l1_sgu_b_s, l1_sgu_w_out,
              l1_ffn2_norm, l1_ffn2_w_in, l1_ffn2_w_out,
              l2_ffn1_norm, l2_ffn1_w_in, l2_ffn1_w_out, l2_mix_norm,
              l2_swa_w_qkv, l2_swa_b_qkv, l2_swa_sinks, l2_swa_w_out, l2_swa_b_out,
              l2_ffn2_norm, l2_ffn2_w_in, l2_ffn2_w_out,
              l3_ffn1_norm, l3_ffn1_w_in, l3_ffn1_w_out, l3_mix_norm,
              l3_gla_w_in, l3_gla_w_gk_up, l3_gla_b_gk, l3_gla_o_norm, l3_gla_w_out,
              l3_ffn2_norm, l3_ffn2_w_in, l3_ffn2_w_out,
              final_norm):
    cos, sin = rope_tables(positions)
    ffn1 = [(l0_ffn1_norm, l0_ffn1_w_in, l0_ffn1_w_out), (l1_ffn1_norm, l1_ffn1_w_in, l1_ffn1_w_out),
            (l2_ffn1_norm, l2_ffn1_w_in, l2_ffn1_w_out), (l3_ffn1_norm, l3_ffn1_w_in, l3_ffn1_w_out)]
    ffn2 = [(l0_ffn2_norm, l0_ffn2_w_in, l0_ffn2_w_out), (l1_ffn2_norm, l1_ffn2_w_in, l1_ffn2_w_out),
            (l2_ffn2_norm, l2_ffn2_w_in, l2_ffn2_w_out), (l3_ffn2_norm, l3_ffn2_w_in, l3_ffn2_w_out)]
    mix_norms = [l0_mix_norm, l1_mix_norm, l2_mix_norm, l3_mix_norm]
    mixers = [
        lambda t: gla_mixer(t, l0_gla_w_in, l0_gla_w_gk_up, l0_gla_b_gk, l0_gla_o_norm, l0_gla_w_out),
        lambda t: sgu_mixer(t, l1_sgu_w_in, l1_sgu_ln_gain, l1_sgu_ln_bias, l1_sgu_w_s, l1_sgu_b_s, l1_sgu_w_out),
        lambda t: swa_mixer(t, cos, sin, l2_swa_w_qkv, l2_swa_b_qkv, l2_swa_sinks, l2_swa_w_out, l2_swa_b_out),
        lambda t: gla_mixer(t, l3_gla_w_in, l3_gla_w_gk_up, l3_gla_b_gk, l3_gla_o_norm, l3_gla_w_out),
    ]
    for i in range(DEPTH):
        n1, wi1, wo1 = ffn1[i]
        n2, wi2, wo2 = ffn2[i]
        x = x + 0.5 * swiglu_ffn(rms_norm(x, n1), wi1, wo1)
        x = x + mixers[i](rms_norm(x, mix_norms[i]))
        x = x + 0.5 * swiglu_ffn(rms_norm(x, n2), wi2, wo2)
    return rms_norm(x, final_norm)
```

```python
import functools

import jax
import jax.numpy as jnp
from jax import lax
from jax.experimental import pallas as pl
from jax.experimental.pallas import tpu as pltpu

F32 = jnp.float32
BF16 = jnp.bfloat16

NORM_EPS = 1e-5
LANES = 128
VMEM_LIMIT_BYTES = 56 * 1024 * 1024

GLA_HEADS = 4
GLA_HEAD_K = 128
GLA_HEAD_V = 256
GLA_DK = GLA_HEADS * GLA_HEAD_K
GLA_DV = GLA_HEADS * GLA_HEAD_V
GLA_GATE_RANK = 16
GLA_GATE_NORMALIZER = 16.0
GLA_CHUNK = 64
SGU_GROUPS = 8
SGU_CHUNK = 128
SWA_HEADS = 16
SWA_KV_HEADS = 2
SWA_HEAD_DIM = 64
SWA_GROUP = SWA_HEADS // SWA_KV_HEADS
SWA_BLOCK = 128
ROPE_DIM = SWA_HEAD_DIM // 4
ROPE_HALF = ROPE_DIM // 2
ROPE_THETA = 500000.0
MASKED_SCORE = -1e30


def _dot(a, b):
    return jnp.dot(a, b, preferred_element_type=F32)


def _dot_nt(a, b):
    return lax.dot_general(a, b, (((1,), (1,)), ((), ())), preferred_element_type=F32)


def _dot_tn(a, b):
    return lax.dot_general(a, b, (((0,), (0,)), ((), ())), preferred_element_type=F32)


def _rms_norm(x, gain):
    return x * lax.rsqrt(jnp.mean(x * x, axis=-1, keepdims=True) + NORM_EPS) * gain


def _silu(x):
    return x * (1.0 / (1.0 + jnp.exp(-x)))


def _resident(shape):
    zeros = (0,) * len(shape)
    return pl.BlockSpec(shape, lambda *_: zeros, pipeline_mode=pl.Buffered(1))


def _compiler_params(semantics):
    return pltpu.CompilerParams(dimension_semantics=semantics,
                                vmem_limit_bytes=VMEM_LIMIT_BYTES)


FFN_ROWS = 512
FFN_COLS = 256


def _ffn_body(x_ref, g_ref, wi_ref, wo_ref, fg_ref, o_ref, act_ref, *, d_ff, final):
    x = x_ref[...]
    xn = _rms_norm(x, g_ref[...]).astype(BF16)
    for c in range(d_ff // FFN_COLS):
        lo = c * FFN_COLS
        gate = _dot(xn, wi_ref[:, lo:lo + FFN_COLS])
        up = _dot(xn, wi_ref[:, d_ff + lo:d_ff + lo + FFN_COLS])
        act_ref[:, lo:lo + FFN_COLS] = (_silu(gate) * up).astype(BF16)
    y = x + 0.5 * _dot(act_ref[...], wo_ref[...])
    if final:
        y = _rms_norm(y, fg_ref[...])
    o_ref[...] = y


def _ffn(x2, norm, w_in, w_out, final_gain, *, final):
    t, d = x2.shape
    d_ff = w_out.shape[0]
    rows = min(FFN_ROWS, t)
    body = functools.partial(_ffn_body, d_ff=d_ff, final=final)
    return pl.pallas_call(
        body,
        grid=(t // rows,),
        in_specs=[
            pl.BlockSpec((rows, d), lambda i: (i, 0)),
            _resident((1, d)),
            _resident((d, 2 * d_ff)),
            _resident((d_ff, d)),
            _resident((1, d)),
        ],
        out_specs=pl.BlockSpec((rows, d), lambda i: (i, 0)),
        out_shape=jax.ShapeDtypeStruct((t, d), F32),
        scratch_shapes=[pltpu.VMEM((rows, d_ff), BF16)],
        compiler_params=_compiler_params(("parallel",)),
        name="ffn_final" if final else "ffn",
    )(x2, norm.reshape(1, d), w_in.astype(BF16), w_out.astype(BF16), final_gain.reshape(1, d))


GLA_ROWS = 512


def _gla_body(x_ref, g_ref, wq_ref, wgk_ref, wup_ref, bgk_ref, on_ref, wo_ref, o_ref,
              state_ref, qd_ref, ki_ref, ks_ref, v_ref, dec_ref, oacc_ref, *, rows):
    n_chunks = rows // GLA_CHUNK

    @pl.when(pl.program_id(1) == 0)
    def _():
        state_ref[...] = jnp.zeros_like(state_ref)

    x = x_ref[...]
    xn = _rms_norm(x, g_ref[...]).astype(BF16)
    q = _dot(xn, wq_ref[:, 0:GLA_DK]) * (GLA_HEAD_K ** -0.5)
    k = _dot(xn, wq_ref[:, GLA_DK:2 * GLA_DK])
    v_ref[...] = _dot(xn, wq_ref[:, 2 * GLA_DK:2 * GLA_DK + GLA_DV]).astype(BF16)
    gk_low = _dot(xn, wgk_ref[...])
    z = _dot(gk_low.astype(BF16), wup_ref[...]) + bgk_ref[...]
    log_a = (jnp.minimum(z, 0.0) - jnp.log1p(jnp.exp(-jnp.abs(z)))) * (1.0 / GLA_GATE_NORMALIZER)

    ri = lax.broadcasted_iota(jnp.int32, (GLA_CHUNK, GLA_CHUNK), 0)
    ci = lax.broadcasted_iota(jnp.int32, (GLA_CHUNK, GLA_CHUNK), 1)
    causal = ri >= ci
    tril = causal.astype(F32)
    for c in range(n_chunks):
        r0 = c * GLA_CHUNK
        b = jnp.dot(tril, log_a[r0:r0 + GLA_CHUNK], precision=lax.Precision.HIGHEST,
                    preferred_element_type=F32)
        b_last = b[GLA_CHUNK - 1:GLA_CHUNK]
        q_c = q[r0:r0 + GLA_CHUNK]
        k_c = k[r0:r0 + GLA_CHUNK]
        qd_ref[r0:r0 + GLA_CHUNK] = (q_c * jnp.exp(b)).astype(BF16)
        ki_ref[r0:r0 + GLA_CHUNK] = (k_c * jnp.exp(-b)).astype(BF16)
        ks_ref[r0:r0 + GLA_CHUNK] = (k_c * jnp.exp(b_last - b)).astype(BF16)
        dec_ref[c] = jnp.exp(b_last)

    def chunk_step(c, carry):
        r0 = pl.multiple_of(c * GLA_CHUNK, GLA_CHUNK)
        rs = pl.ds(r0, GLA_CHUNK)
        dec = dec_ref[c]
        for h in range(GLA_HEADS):
            ks_ = slice(h * GLA_HEAD_K, (h + 1) * GLA_HEAD_K)
            vs_ = slice(h * GLA_HEAD_V, (h + 1) * GLA_HEAD_V)
            qd = qd_ref[rs, ks_]
            v_h = v_ref[rs, vs_]
            st = state_ref[h]
            attn = jnp.where(causal, _dot_nt(qd, ki_ref[rs, ks_]), 0.0)
            oacc_ref[rs, vs_] = _dot(attn.astype(BF16), v_h) + _dot_nt(qd, st.astype(BF16))
            state_ref[h] = st * dec[:, ks_] + _dot_tn(v_h, ks_ref[rs, ks_])
        return carry

    lax.fori_loop(0, n_chunks, chunk_step, 0)

    r = _dot(xn, wq_ref[:, 2 * GLA_DK + GLA_DV:2 * GLA_DK + 2 * GLA_DV])
    gain = on_ref[...]
    for h in range(GLA_HEADS):
        vs_ = slice(h * GLA_HEAD_V, (h + 1) * GLA_HEAD_V)
        o_h = _rms_norm(oacc_ref[:, vs_], gain)
        oacc_ref[:, vs_] = o_h * _silu(r[:, vs_])
    o_ref[...] = x + _dot(oacc_ref[...].astype(BF16), wo_ref[...])


def _gla(x3, norm, w_in, w_gk_up, b_gk, o_norm, w_out):
    bsz, s, d = x3.shape
    rows = min(GLA_ROWS, s)
    n_qkvr = 2 * GLA_DK + 2 * GLA_DV
    w_qkvr = w_in[:, :n_qkvr].astype(BF16)
    w_gk = jnp.pad(w_in[:, n_qkvr:], ((0, 0), (0, LANES - GLA_GATE_RANK))).astype(BF16)
    w_up = jnp.pad(w_gk_up, ((0, LANES - GLA_GATE_RANK), (0, 0))).astype(BF16)
    body = functools.partial(_gla_body, rows=rows)
    xspec = pl.BlockSpec((None, rows, d), lambda b, i: (b, i, 0))
    return pl.pallas_call(
        body,
        grid=(bsz, s // rows),
        in_specs=[
            xspec,
            _resident((1, d)),
            _resident((d, n_qkvr)),
            _resident((d, LANES)),
            _resident((LANES, GLA_DK)),
            _resident((1, GLA_DK)),
            _resident((1, GLA_HEAD_V)),
            _resident((GLA_DV, d)),
        ],
        out_specs=xspec,
        out_shape=jax.ShapeDtypeStruct((bsz, s, d), F32),
        scratch_shapes=[
            pltpu.VMEM((GLA_HEADS, GLA_HEAD_V, GLA_HEAD_K), F32),
            pltpu.VMEM((rows, GLA_DK), BF16),
            pltpu.VMEM((rows, GLA_DK), BF16),
            pltpu.VMEM((rows, GLA_DK), BF16),
            pltpu.VMEM((rows, GLA_DV), BF16),
            pltpu.VMEM((rows // GLA_CHUNK, 1, GLA_DK), F32),
            pltpu.VMEM((rows, GLA_DV), F32),
        ],
        compiler_params=_compiler_params(("parallel", "arbitrary")),
        name="gla",
    )(x3, norm.reshape(1, d), w_qkvr, w_gk, w_up, b_gk.reshape(1, GLA_DK),
      o_norm.reshape(1, GLA_HEAD_V), w_out.astype(BF16))


SGU_ROWS = 256
SGU_COLS = 384


def _gelu(x):
    return 0.5 * x * (1.0 + lax.erf(x * (2.0 ** -0.5)))


def _sgu_body(x_ref, g_ref, wi_ref, lng_ref, lnb_ref, ws_ref, bs_ref, wo_ref, o_ref,
              v_ref, uv_ref, *, rows, half):
    x = x_ref[...]
    xn = _rms_norm(x, g_ref[...]).astype(BF16)
    n_col = half // SGU_COLS

    total = jnp.zeros((rows, 1), F32)
    for c in range(n_col):
        lo = c * SGU_COLS
        v_c = _gelu(_dot(xn, wi_ref[:, half + lo:half + lo + SGU_COLS]))
        v_ref[:, lo:lo + SGU_COLS] = v_c
        total = total + jnp.sum(v_c, axis=-1, keepdims=True)
    mu = total * (1.0 / half)
    sq = jnp.zeros((rows, 1), F32)
    for c in range(n_col):
        lo = c * SGU_COLS
        d_c = v_ref[:, lo:lo + SGU_COLS] - mu
        sq = sq + jnp.sum(d_c * d_c, axis=-1, keepdims=True)
    inv = lax.rsqrt(sq * (1.0 / half) + NORM_EPS)

    ri = lax.broadcasted_iota(jnp.int32, (SGU_CHUNK, SGU_CHUNK), 0)
    ci = lax.broadcasted_iota(jnp.int32, (SGU_CHUNK, SGU_CHUNK), 1)
    causal = ri >= ci
    group_dim = half // SGU_GROUPS
    for g in range(SGU_GROUPS):
        lo = g * group_dim
        cols = slice(lo, lo + group_dim)
        vn = ((v_ref[:, cols] - mu) * inv * lng_ref[:, cols] + lnb_ref[:, cols]).astype(BF16)
        w_g = jnp.where(causal, ws_ref[g], 0.0).astype(BF16)
        bias = bs_ref[:, g:g + 1]
        u_g = _gelu(_dot(xn, wi_ref[:, cols]))
        for j in range(rows // SGU_CHUNK):
            rs = slice(j * SGU_CHUNK, (j + 1) * SGU_CHUNK)
            mixed = _dot(w_g, vn[rs]) + bias
            uv_ref[rs, cols] = (u_g[rs] * mixed).astype(BF16)
    o_ref[...] = x + _dot(uv_ref[...], wo_ref[...])


def _sgu(x2, norm, w_in, ln_gain, ln_bias, w_s, b_s, w_out):
    t, d = x2.shape
    half = w_out.shape[0]
    rows = min(SGU_ROWS, t)
    body = functools.partial(_sgu_body, rows=rows, half=half)
    return pl.pallas_call(
        body,
        grid=(t // rows,),
        in_specs=[
            pl.BlockSpec((rows, d), lambda i: (i, 0)),
            _resident((1, d)),
            _resident((d, 2 * half)),
            _resident((1, half)),
            _resident((1, half)),
            _resident((SGU_GROUPS, SGU_CHUNK, SGU_CHUNK)),
            _resident((SGU_CHUNK, SGU_GROUPS)),
            _resident((half, d)),
        ],
        out_specs=pl.BlockSpec((rows, d), lambda i: (i, 0)),
        out_shape=jax.ShapeDtypeStruct((t, d), F32),
        scratch_shapes=[
            pltpu.VMEM((rows, half), F32),
            pltpu.VMEM((rows, half), BF16),
        ],
        compiler_params=_compiler_params(("parallel",)),
        name="sgu",
    )(x2, norm.reshape(1, d), w_in.astype(BF16), ln_gain.reshape(1, half),
      ln_bias.reshape(1, half), w_s, b_s.T, w_out.astype(BF16))


SWA_ROWS = 512


def _swa_body(x_ref, pos_ref, g_ref, wqkv_ref, bqkv_ref, invf_ref, sinks_ref, wo_ref, bo_ref,
              o_ref, k_ref, v_ref, attn_ref, *, rows):
    d_q = SWA_HEADS * SWA_HEAD_DIM
    d_kv = SWA_KV_HEADS * SWA_HEAD_DIM
    step = pl.program_id(1)

    @pl.when(step == 0)
    def _():
        k_ref[0:SWA_BLOCK] = jnp.zeros((SWA_BLOCK, d_kv), F32)
        v_ref[0:SWA_BLOCK] = jnp.zeros((SWA_BLOCK, d_kv), F32)

    @pl.when(step > 0)
    def _():
        k_ref[0:SWA_BLOCK] = k_ref[rows:rows + SWA_BLOCK]
        v_ref[0:SWA_BLOCK] = v_ref[rows:rows + SWA_BLOCK]

    x = x_ref[...]
    xn = _rms_norm(x, g_ref[...]).astype(BF16)

    ang = pos_ref[...].astype(F32) * invf_ref[...]
    cos_t = jnp.cos(ang)
    sin_t = jnp.sin(ang)
    head_lane = lax.broadcasted_iota(jnp.int32, (1, LANES), 1) % SWA_HEAD_DIM
    sin_up = jnp.where((head_lane >= ROPE_HALF) & (head_lane < ROPE_DIM), sin_t, 0.0)
    sin_dn = jnp.where(head_lane < ROPE_HALF, -sin_t, 0.0)

    def rope(t, rs):
        return (t * cos_t[rs] + pltpu.roll(t, ROPE_HALF, 1) * sin_up[rs]
                + pltpu.roll(t, LANES - ROPE_HALF, 1) * sin_dn[rs])

    k_new = rope(_dot(xn, wqkv_ref[:, d_q:d_q + d_kv]) + bqkv_ref[:, d_q:d_q + d_kv], slice(None))
    k_ref[SWA_BLOCK:SWA_BLOCK + rows] = k_new
    v_ref[SWA_BLOCK:SWA_BLOCK + rows] = (
        _dot(xn, wqkv_ref[:, d_q + d_kv:d_q + 2 * d_kv]) + bqkv_ref[:, d_q + d_kv:d_q + 2 * d_kv])

    lane = lax.broadcasted_iota(jnp.int32, (1, LANES), 1)
    low_half = lane < SWA_HEAD_DIM
    qi = lax.broadcasted_iota(jnp.int32, (SWA_BLOCK, 2 * SWA_BLOCK), 0)
    kj = lax.broadcasted_iota(jnp.int32, (SWA_BLOCK, 2 * SWA_BLOCK), 1)
    delta = qi + SWA_BLOCK - kj
    in_window = (delta >= 0) & (delta < SWA_BLOCK)

    heads_per_tile = LANES // SWA_HEAD_DIM
    for j in range(rows // SWA_BLOCK):
        rs = slice(j * SWA_BLOCK, (j + 1) * SWA_BLOCK)
        band = slice(j * SWA_BLOCK, (j + 2) * SWA_BLOCK)
        k_band = k_ref[band]
        v_band = v_ref[band]
        k_swap = pltpu.roll(k_band, SWA_HEAD_DIM, 1)
        v_swap = pltpu.roll(v_band, SWA_HEAD_DIM, 1)
        k_dup = [jnp.where(low_half, k_band, k_swap).astype(BF16),
                 jnp.where(low_half, k_swap, k_band).astype(BF16)]
        v_dup = [jnp.where(low_half, v_band, v_swap).astype(BF16),
                 jnp.where(low_half, v_swap, v_band).astype(BF16)]
        if j == 0:
            valid = in_window & (kj >= jnp.where(step > 0, 0, SWA_BLOCK))
        else:
            valid = in_window
        for p in range(SWA_HEADS // heads_per_tile):
            cols = slice(p * LANES, (p + 1) * LANES)
            kvh = (p * heads_per_tile) // SWA_GROUP
            q_pair = rope(_dot(xn[rs], wqkv_ref[:, cols]) + bqkv_ref[:, cols], rs) * (SWA_HEAD_DIM ** -0.5)
            outs = []
            for e in range(heads_per_tile):
                in_head = low_half if e == 0 else jnp.logical_not(low_half)
                q_h = jnp.where(in_head, q_pair, 0.0).astype(BF16)
                scores = jnp.where(valid, _dot_nt(q_h, k_dup[kvh]), MASKED_SCORE)
                sink = sinks_ref[p * heads_per_tile + e]
                m = jnp.maximum(jnp.max(scores, axis=-1, keepdims=True), sink)
                pr = jnp.exp(scores - m)
                denom = jnp.sum(pr, axis=-1, keepdims=True) + jnp.exp(sink - m)
                outs.append(_dot(pr.astype(BF16), v_dup[kvh]) / denom)
            attn_ref[rs, cols] = jnp.where(low_half, outs[0], outs[1]).astype(BF16)
    o_ref[...] = x + _dot(attn_ref[...], wo_ref[...]) + bo_ref[...]


def _swa(x3, positions, norm, w_qkv, b_qkv, sinks, w_out, b_out):
    bsz, s, d = x3.shape
    rows = min(SWA_ROWS, s)
    d_q = SWA_HEADS * SWA_HEAD_DIM
    d_kv = SWA_KV_HEADS * SWA_HEAD_DIM
    n_qkv = d_q + 2 * d_kv
    inv_freq = ROPE_THETA ** (-jnp.arange(0, ROPE_DIM, 2, dtype=F32) / ROPE_DIM)
    head_lane = jnp.arange(LANES) % SWA_HEAD_DIM
    invf = jnp.where(head_lane < ROPE_DIM, inv_freq[head_lane % ROPE_HALF], 0.0).reshape(1, LANES)
    body = functools.partial(_swa_body, rows=rows)
    xspec = pl.BlockSpec((None, rows, d), lambda b, i: (b, i, 0))
    return pl.pallas_call(
        body,
        grid=(bsz, s // rows),
        in_specs=[
            xspec,
            pl.BlockSpec((None, rows, 1), lambda b, i: (b, i, 0)),
            _resident((1, d)),
            _resident((d, n_qkv)),
            _resident((1, n_qkv)),
            _resident((1, LANES)),
            pl.BlockSpec(memory_space=pltpu.SMEM),
            _resident((d_q, d)),
            _resident((1, d)),
        ],
        out_specs=xspec,
        out_shape=jax.ShapeDtypeStruct((bsz, s, d), F32),
        scratch_shapes=[
            pltpu.VMEM((SWA_BLOCK + rows, d_kv), F32),
            pltpu.VMEM((SWA_BLOCK + rows, d_kv), F32),
            pltpu.VMEM((rows, d_q), BF16),
        ],
        compiler_params=_compiler_params(("parallel", "arbitrary")),
        name="swa",
    )(x3, positions.reshape(bsz, s, 1), norm.reshape(1, d), w_qkv.astype(BF16),
      b_qkv.reshape(1, n_qkv), invf, sinks, w_out.astype(BF16), b_out.reshape(1, d))


def kernel(x, positions, l0_ffn1_norm, l0_ffn1_w_in, l0_ffn1_w_out, l0_mix_norm, l0_gla_w_in, l0_gla_w_gk_up, l0_gla_b_gk, l0_gla_o_norm, l0_gla_w_out, l0_ffn2_norm, l0_ffn2_w_in, l0_ffn2_w_out, l1_ffn1_norm, l1_ffn1_w_in, l1_ffn1_w_out, l1_mix_norm, l1_sgu_w_in, l1_sgu_ln_gain, l1_sgu_ln_bias, l1_sgu_w_s, l1_sgu_b_s, l1_sgu_w_out, l1_ffn2_norm, l1_ffn2_w_in, l1_ffn2_w_out, l2_ffn1_norm, l2_ffn1_w_in, l2_ffn1_w_out, l2_mix_norm, l2_swa_w_qkv, l2_swa_b_qkv, l2_swa_sinks, l2_swa_w_out, l2_swa_b_out, l2_ffn2_norm, l2_ffn2_w_in, l2_ffn2_w_out, l3_ffn1_norm, l3_ffn1_w_in, l3_ffn1_w_out, l3_mix_norm, l3_gla_w_in, l3_gla_w_gk_up, l3_gla_b_gk, l3_gla_o_norm, l3_gla_w_out, l3_ffn2_norm, l3_ffn2_w_in, l3_ffn2_w_out, final_norm):
    bsz, s, d = x.shape

    def ffn(t, norm, w_in, w_out, final=False):
        return _ffn(t.reshape(bsz * s, d), norm, w_in, w_out, final_norm, final=final).reshape(bsz, s, d)

    x = ffn(x, l0_ffn1_norm, l0_ffn1_w_in, l0_ffn1_w_out)
    x = _gla(x, l0_mix_norm, l0_gla_w_in, l0_gla_w_gk_up, l0_gla_b_gk, l0_gla_o_norm, l0_gla_w_out)
    x = ffn(x, l0_ffn2_norm, l0_ffn2_w_in, l0_ffn2_w_out)
    x = ffn(x, l1_ffn1_norm, l1_ffn1_w_in, l1_ffn1_w_out)
    x = _sgu(x.reshape(bsz * s, d), l1_mix_norm, l1_sgu_w_in, l1_sgu_ln_gain, l1_sgu_ln_bias,
             l1_sgu_w_s, l1_sgu_b_s, l1_sgu_w_out).reshape(bsz, s, d)
    x = ffn(x, l1_ffn2_norm, l1_ffn2_w_in, l1_ffn2_w_out)
    x = ffn(x, l2_ffn1_norm, l2_ffn1_w_in, l2_ffn1_w_out)
    x = _swa(x, positions, l2_mix_norm, l2_swa_w_qkv, l2_swa_b_qkv, l2_swa_sinks, l2_swa_w_out, l2_swa_b_out)
    x = ffn(x, l2_ffn2_norm, l2_ffn2_w_in, l2_ffn2_w_out)
    x = ffn(x, l3_ffn1_norm, l3_ffn1_w_in, l3_ffn1_w_out)
    x = _gla(x, l3_mix_norm, l3_gla_w_in, l3_gla_w_gk_up, l3_gla_b_gk, l3_gla_o_norm, l3_gla_w_out)
    x = ffn(x, l3_ffn2_norm, l3_ffn2_w_in, l3_ffn2_w_out, final=True)
    return x
```

```python
import functools

import jax
import jax.numpy as jnp
from jax import lax
from jax.experimental import pallas as pl
from jax.experimental.pallas import tpu as pltpu

F32 = jnp.float32
BF16 = jnp.bfloat16

NORM_EPS = 1e-5
LANES = 128
VMEM_LIMIT_BYTES = 56 * 1024 * 1024

GLA_HEADS = 4
GLA_HEAD_K = 128
GLA_HEAD_V = 256
GLA_DK = GLA_HEADS * GLA_HEAD_K
GLA_DV = GLA_HEADS * GLA_HEAD_V
GLA_GATE_RANK = 16
GLA_GATE_NORMALIZER = 16.0
GLA_CHUNK = 64
SGU_GROUPS = 8
SGU_CHUNK = 128
SWA_HEADS = 16
SWA_KV_HEADS = 2
SWA_HEAD_DIM = 64
SWA_GROUP = SWA_HEADS // SWA_KV_HEADS
SWA_BLOCK = 128
ROPE_DIM = SWA_HEAD_DIM // 4
ROPE_HALF = ROPE_DIM // 2
ROPE_THETA = 500000.0
MASKED_SCORE = -1e30


def _dot(a, b):
    return jnp.dot(a, b, preferred_element_type=F32)


def _dot_nt(a, b):
    return lax.dot_general(a, b, (((1,), (1,)), ((), ())), preferred_element_type=F32)


def _dot_tn(a, b):
    return lax.dot_general(a, b, (((0,), (0,)), ((), ())), preferred_element_type=F32)


def _rms_norm(x, gain):
    return x * lax.rsqrt(jnp.mean(x * x, axis=-1, keepdims=True) + NORM_EPS) * gain


def _silu(x):
    return x * (1.0 / (1.0 + jnp.exp(-x)))


def _resident(shape):
    zeros = (0,) * len(shape)
    return pl.BlockSpec(shape, lambda *_: zeros, pipeline_mode=pl.Buffered(1))


def _compiler_params(semantics):
    return pltpu.CompilerParams(dimension_semantics=semantics,
                                vmem_limit_bytes=VMEM_LIMIT_BYTES)


FFN_ROWS = 1024
FFN_COLS = 256


def _ffn_body(x_ref, g_ref, wi_ref, wo_ref, fg_ref, o_ref, act_ref, *, d_ff, final):
    x = x_ref[...]
    xn = _rms_norm(x, g_ref[...]).astype(BF16)
    for c in range(d_ff // FFN_COLS):
        lo = c * FFN_COLS
        gate = _dot(xn, wi_ref[:, lo:lo + FFN_COLS])
        up = _dot(xn, wi_ref[:, d_ff + lo:d_ff + lo + FFN_COLS])
        act_ref[:, lo:lo + FFN_COLS] = (_silu(gate) * up).astype(BF16)
    y = x + 0.5 * _dot(act_ref[...], wo_ref[...])
    if final:
        y = _rms_norm(y, fg_ref[...])
    o_ref[...] = y


def _ffn(x2, norm, w_in, w_out, final_gain, *, final):
    t, d = x2.shape
    d_ff = w_out.shape[0]
    rows = min(FFN_ROWS, t)
    body = functools.partial(_ffn_body, d_ff=d_ff, final=final)
    return pl.pallas_call(
        body,
        grid=(t // rows,),
        in_specs=[
            pl.BlockSpec((rows, d), lambda i: (i, 0)),
            _resident((1, d)),
            _resident((d, 2 * d_ff)),
            _resident((d_ff, d)),
            _resident((1, d)),
        ],
        out_specs=pl.BlockSpec((rows, d), lambda i: (i, 0)),
        out_shape=jax.ShapeDtypeStruct((t, d), F32),
        scratch_shapes=[pltpu.VMEM((rows, d_ff), BF16)],
        compiler_params=_compiler_params(("parallel",)),
        name="ffn_final" if final else "ffn",
    )(x2, norm.reshape(1, d), w_in.astype(BF16), w_out.astype(BF16), final_gain.reshape(1, d))


GLA_ROWS = 512


def _gla_body(x_ref, g_ref, wq_ref, wgk_ref, wup_ref, bgk_ref, on_ref, wo_ref, o_ref,
              state_ref, qd_ref, ki_ref, ks_ref, v_ref, oacc_ref, *, rows):
    n_chunks = rows // GLA_CHUNK

    @pl.when(pl.program_id(1) == 0)
    def _():
        state_ref[...] = jnp.zeros_like(state_ref)

    x = x_ref[...]
    xn = _rms_norm(x, g_ref[...]).astype(BF16)
    q = _dot(xn, wq_ref[:, 0:GLA_DK]) * (GLA_HEAD_K ** -0.5)
    k = _dot(xn, wq_ref[:, GLA_DK:2 * GLA_DK])
    v_ref[...] = _dot(xn, wq_ref[:, 2 * GLA_DK:2 * GLA_DK + GLA_DV]).astype(BF16)
    gk_low = _dot(xn, wgk_ref[...])
    z = _dot(gk_low.astype(BF16), wup_ref[...]) + bgk_ref[...]
    log_a = (jnp.minimum(z, 0.0) - jnp.log1p(jnp.exp(-jnp.abs(z)))) * (1.0 / GLA_GATE_NORMALIZER)

    ri = lax.broadcasted_iota(jnp.int32, (GLA_CHUNK, GLA_CHUNK), 0)
    ci = lax.broadcasted_iota(jnp.int32, (GLA_CHUNK, GLA_CHUNK), 1)
    causal = ri >= ci
    tril = causal.astype(BF16)
    tril3 = jnp.concatenate([tril, tril, tril], axis=1)
    la_hi = log_a.astype(BF16)
    rest = log_a - la_hi.astype(F32)
    la_mid = rest.astype(BF16)
    la_lo = (rest - la_mid.astype(F32)).astype(BF16)
    decays = []
    for c in range(n_chunks):
        rs = slice(c * GLA_CHUNK, (c + 1) * GLA_CHUNK)
        b = _dot(tril3, jnp.concatenate([la_hi[rs], la_mid[rs], la_lo[rs]], axis=0))
        b_last = b[GLA_CHUNK - 1:GLA_CHUNK]
        qd_ref[rs] = (q[rs] * jnp.exp(b)).astype(BF16)
        ki_ref[rs] = (k[rs] * jnp.exp(-b)).astype(BF16)
        ks_ref[rs] = (k[rs] * jnp.exp(b_last - b)).astype(BF16)
        decays.append(jnp.exp(b_last))

    states = [state_ref[h] for h in range(GLA_HEADS)]
    for c in range(n_chunks):
        rs = slice(c * GLA_CHUNK, (c + 1) * GLA_CHUNK)
        for h in range(GLA_HEADS):
            ks_ = slice(h * GLA_HEAD_K, (h + 1) * GLA_HEAD_K)
            vs_ = slice(h * GLA_HEAD_V, (h + 1) * GLA_HEAD_V)
            qd = qd_ref[rs, ks_]
            v_h = v_ref[rs, vs_]
            attn = jnp.where(causal, _dot_nt(qd, ki_ref[rs, ks_]), 0.0)
            oacc_ref[rs, vs_] = _dot(attn.astype(BF16), v_h) + _dot_nt(qd, states[h].astype(BF16))
            states[h] = states[h] * decays[c][:, ks_] + _dot_tn(v_h, ks_ref[rs, ks_])
    for h in range(GLA_HEADS):
        state_ref[h] = states[h]

    r = _dot(xn, wq_ref[:, 2 * GLA_DK + GLA_DV:2 * GLA_DK + 2 * GLA_DV])
    gain = on_ref[...]
    for h in range(GLA_HEADS):
        vs_ = slice(h * GLA_HEAD_V, (h + 1) * GLA_HEAD_V)
        o_h = _rms_norm(oacc_ref[:, vs_], gain)
        oacc_ref[:, vs_] = o_h * _silu(r[:, vs_])
    o_ref[...] = x + _dot(oacc_ref[...].astype(BF16), wo_ref[...])


def _gla(x3, norm, w_in, w_gk_up, b_gk, o_norm, w_out):
    bsz, s, d = x3.shape
    rows = min(GLA_ROWS, s)
    n_qkvr = 2 * GLA_DK + 2 * GLA_DV
    w_qkvr = w_in[:, :n_qkvr].astype(BF16)
    w_gk = jnp.pad(w_in[:, n_qkvr:], ((0, 0), (0, LANES - GLA_GATE_RANK))).astype(BF16)
    w_up = jnp.pad(w_gk_up, ((0, LANES - GLA_GATE_RANK), (0, 0))).astype(BF16)
    body = functools.partial(_gla_body, rows=rows)
    xspec = pl.BlockSpec((None, rows, d), lambda b, i: (b, i, 0))
    return pl.pallas_call(
        body,
        grid=(bsz, s // rows),
        in_specs=[
            xspec,
            _resident((1, d)),
            _resident((d, n_qkvr)),
            _resident((d, LANES)),
            _resident((LANES, GLA_DK)),
            _resident((1, GLA_DK)),
            _resident((1, GLA_HEAD_V)),
            _resident((GLA_DV, d)),
        ],
        out_specs=xspec,
        out_shape=jax.ShapeDtypeStruct((bsz, s, d), F32),
        scratch_shapes=[
            pltpu.VMEM((GLA_HEADS, GLA_HEAD_V, GLA_HEAD_K), F32),
            pltpu.VMEM((rows, GLA_DK), BF16),
            pltpu.VMEM((rows, GLA_DK), BF16),
            pltpu.VMEM((rows, GLA_DK), BF16),
            pltpu.VMEM((rows, GLA_DV), BF16),
            pltpu.VMEM((rows, GLA_DV), F32),
        ],
        compiler_params=_compiler_params(("parallel", "arbitrary")),
        name="gla",
    )(x3, norm.reshape(1, d), w_qkvr, w_gk, w_up, b_gk.reshape(1, GLA_DK),
      o_norm.reshape(1, GLA_HEAD_V), w_out.astype(BF16))


SGU_ROWS = 512
SGU_COLS = 384


def _gelu(x):
    return 0.5 * x * (1.0 + lax.erf(x * (2.0 ** -0.5)))


def _sgu_body(x_ref, g_ref, wi_ref, lng_ref, lnb_ref, ws_ref, bs_ref, wo_ref, o_ref,
              v_ref, uv_ref, *, rows, half):
    x = x_ref[...]
    xn = _rms_norm(x, g_ref[...]).astype(BF16)
    n_col = half // SGU_COLS

    total = jnp.zeros((rows, 1), F32)
    for c in range(n_col):
        lo = c * SGU_COLS
        v_c = _gelu(_dot(xn, wi_ref[:, half + lo:half + lo + SGU_COLS]))
        v_ref[:, lo:lo + SGU_COLS] = v_c
        total = total + jnp.sum(v_c, axis=-1, keepdims=True)
    mu = total * (1.0 / half)
    sq = jnp.zeros((rows, 1), F32)
    for c in range(n_col):
        lo = c * SGU_COLS
        d_c = v_ref[:, lo:lo + SGU_COLS] - mu
        sq = sq + jnp.sum(d_c * d_c, axis=-1, keepdims=True)
    inv = lax.rsqrt(sq * (1.0 / half) + NORM_EPS)

    ri = lax.broadcasted_iota(jnp.int32, (SGU_CHUNK, SGU_CHUNK), 0)
    ci = lax.broadcasted_iota(jnp.int32, (SGU_CHUNK, SGU_CHUNK), 1)
    causal = ri >= ci
    group_dim = half // SGU_GROUPS
    for g in range(SGU_GROUPS):
        lo = g * group_dim
        cols = slice(lo, lo + group_dim)
        vn = ((v_ref[:, cols] - mu) * inv * lng_ref[:, cols] + lnb_ref[:, cols]).astype(BF16)
        w_g = jnp.where(causal, ws_ref[g], 0.0).astype(BF16)
        bias = bs_ref[:, g:g + 1]
        u_g = _gelu(_dot(xn, wi_ref[:, cols]))
        for j in range(rows // SGU_CHUNK):
            rs = slice(j * SGU_CHUNK, (j + 1) * SGU_CHUNK)
            mixed = _dot(w_g, vn[rs]) + bias
            uv_ref[rs, cols] = (u_g[rs] * mixed).astype(BF16)
    o_ref[...] = x + _dot(uv_ref[...], wo_ref[...])


def _sgu(x2, norm, w_in, ln_gain, ln_bias, w_s, b_s, w_out):
    t, d = x2.shape
    half = w_out.shape[0]
    rows = min(SGU_ROWS, t)
    body = functools.partial(_sgu_body, rows=rows, half=half)
    return pl.pallas_call(
        body,
        grid=(t // rows,),
        in_specs=[
            pl.BlockSpec((rows, d), lambda i: (i, 0)),
            _resident((1, d)),
            _resident((d, 2 * half)),
            _resident((1, half)),
            _resident((1, half)),
            _resident((SGU_GROUPS, SGU_CHUNK, SGU_CHUNK)),
            _resident((SGU_CHUNK, SGU_GROUPS)),
            _resident((half, d)),
        ],
        out_specs=pl.BlockSpec((rows, d), lambda i: (i, 0)),
        out_shape=jax.ShapeDtypeStruct((t, d), F32),
        scratch_shapes=[
            pltpu.VMEM((rows, half), F32),
            pltpu.VMEM((rows, half), BF16),
        ],
        compiler_params=_compiler_params(("parallel",)),
        name="sgu",
    )(x2, norm.reshape(1, d), w_in.astype(BF16), ln_gain.reshape(1, half),
      ln_bias.reshape(1, half), w_s, b_s.T, w_out.astype(BF16))


SWA_ROWS = 512


def _swa_body(x_ref, pos_ref, g_ref, wqkv_ref, bqkv_ref, invf_ref, sinks_ref, wo_ref, bo_ref,
              o_ref, klo_ref, khi_ref, vlo_ref, vhi_ref, q_ref, p_ref, attn_ref, *, rows):
    d_q = SWA_HEADS * SWA_HEAD_DIM
    d_kv = SWA_KV_HEADS * SWA_HEAD_DIM
    n_blocks = rows // SWA_BLOCK
    tiles_per_kv = SWA_GROUP * SWA_HEAD_DIM // LANES
    step = pl.program_id(1)
    kv_refs = (klo_ref, khi_ref, vlo_ref, vhi_ref)

    @pl.when(step == 0)
    def _():
        for ref in kv_refs:
            ref[:, 0:SWA_BLOCK] = jnp.zeros((SWA_KV_HEADS, SWA_BLOCK, LANES), BF16)

    @pl.when(step > 0)
    def _():
        for ref in kv_refs:
            ref[:, 0:SWA_BLOCK] = ref[:, rows:rows + SWA_BLOCK]

    x = x_ref[...]
    xn = _rms_norm(x, g_ref[...]).astype(BF16)

    ang = pos_ref[...].astype(F32) * invf_ref[...]
    cos_t = jnp.cos(ang)
    sin_t = jnp.sin(ang)
    lane = lax.broadcasted_iota(jnp.int32, (1, LANES), 1)
    head_lane = lane % SWA_HEAD_DIM
    sin_up = jnp.where((head_lane >= ROPE_HALF) & (head_lane < ROPE_DIM), sin_t, 0.0)
    sin_dn = jnp.where(head_lane < ROPE_HALF, -sin_t, 0.0)

    def rope(t):
        return (t * cos_t + pltpu.roll(t, ROPE_HALF, 1) * sin_up
                + pltpu.roll(t, LANES - ROPE_HALF, 1) * sin_dn)

    low_half = lane < SWA_HEAD_DIM
    kv = _dot(xn, wqkv_ref[:, d_q:d_q + 2 * d_kv]) + bqkv_ref[:, d_q:d_q + 2 * d_kv]
    cur = slice(SWA_BLOCK, SWA_BLOCK + rows)
    for t, lo_ref, hi_ref in ((rope(kv[:, :d_kv]), klo_ref, khi_ref), (kv[:, d_kv:], vlo_ref, vhi_ref)):
        swapped = pltpu.roll(t, SWA_HEAD_DIM, 1)
        lo_ref[0, cur] = jnp.where(low_half, t, 0.0).astype(BF16)
        hi_ref[0, cur] = jnp.where(low_half, 0.0, swapped).astype(BF16)
        lo_ref[1, cur] = jnp.where(low_half, swapped, 0.0).astype(BF16)
        hi_ref[1, cur] = jnp.where(low_half, 0.0, t).astype(BF16)

    q_cols = 2 * LANES
    for c in range(d_q // q_cols):
        q2 = _dot(xn, wqkv_ref[:, c * q_cols:(c + 1) * q_cols]) + bqkv_ref[:, c * q_cols:(c + 1) * q_cols]
        for t in range(q_cols // LANES):
            tile = c * (q_cols // LANES) + t
            kvh, slot = divmod(tile, tiles_per_kv)
            q_t = (rope(q2[:, t * LANES:(t + 1) * LANES]) * (SWA_HEAD_DIM ** -0.5)).astype(BF16)
            for j in range(n_blocks):
                q_ref[kvh, j, slot * SWA_BLOCK:(slot + 1) * SWA_BLOCK] = q_t[j * SWA_BLOCK:(j + 1) * SWA_BLOCK]

    qi = lax.broadcasted_iota(jnp.int32, (SWA_BLOCK, 2 * SWA_BLOCK), 0)
    kj = lax.broadcasted_iota(jnp.int32, (SWA_BLOCK, 2 * SWA_BLOCK), 1)
    delta = qi + SWA_BLOCK - kj
    in_window = (delta >= 0) & (delta < SWA_BLOCK)

    for j in range(n_blocks):
        rs = slice(j * SWA_BLOCK, (j + 1) * SWA_BLOCK)
        band = slice(j * SWA_BLOCK, (j + 2) * SWA_BLOCK)
        if j == 0:
            valid = in_window & (kj >= jnp.where(step > 0, 0, SWA_BLOCK))
        else:
            valid = in_window
        for kvh in range(SWA_KV_HEADS):
            qs = q_ref[kvh, j]
            inv = []
            for e, k_ref in enumerate((klo_ref, khi_ref)):
                s_all = _dot_nt(qs, k_ref[kvh, band])
                inv_e = []
                for slot in range(tiles_per_kv):
                    ts = slice(slot * SWA_BLOCK, (slot + 1) * SWA_BLOCK)
                    sink = sinks_ref[(kvh * tiles_per_kv + slot) * 2 + e]
                    scores = jnp.where(valid, s_all[ts], MASKED_SCORE)
                    m = jnp.maximum(jnp.max(scores, axis=-1, keepdims=True), sink)
                    pr = jnp.exp(scores - m)
                    denom = jnp.sum(pr, axis=-1, keepdims=True) + jnp.exp(sink - m)
                    p_ref[e, ts] = pr.astype(BF16)
                    inv_e.append(1.0 / denom)
                inv.append(inv_e)
            out = _dot(p_ref[0], vlo_ref[kvh, band]) + _dot(p_ref[1], vhi_ref[kvh, band])
            for slot in range(tiles_per_kv):
                ts = slice(slot * SWA_BLOCK, (slot + 1) * SWA_BLOCK)
                tile = kvh * tiles_per_kv + slot
                scale = jnp.where(low_half, inv[0][slot], inv[1][slot])
                attn_ref[rs, tile * LANES:(tile + 1) * LANES] = (out[ts] * scale).astype(BF16)
    o_ref[...] = x + _dot(attn_ref[...], wo_ref[...]) + bo_ref[...]


def _swa(x3, positions, norm, w_qkv, b_qkv, sinks, w_out, b_out):
    bsz, s, d = x3.shape
    rows = min(SWA_ROWS, s)
    d_q = SWA_HEADS * SWA_HEAD_DIM
    d_kv = SWA_KV_HEADS * SWA_HEAD_DIM
    n_qkv = d_q + 2 * d_kv
    inv_freq = ROPE_THETA ** (-jnp.arange(0, ROPE_DIM, 2, dtype=F32) / ROPE_DIM)
    head_lane = jnp.arange(LANES) % SWA_HEAD_DIM
    invf = jnp.where(head_lane < ROPE_DIM, inv_freq[head_lane % ROPE_HALF], 0.0).reshape(1, LANES)
    body = functools.partial(_swa_body, rows=rows)
    xspec = pl.BlockSpec((None, rows, d), lambda b, i: (b, i, 0))
    return pl.pallas_call(
        body,
        grid=(bsz, s // rows),
        in_specs=[
            xspec,
            pl.BlockSpec((None, rows, 1), lambda b, i: (b, i, 0)),
            _resident((1, d)),
            _resident((d, n_qkv)),
            _resident((1, n_qkv)),
            _resident((1, LANES)),
            pl.BlockSpec(memory_space=pltpu.SMEM),
            _resident((d_q, d)),
            _resident((1, d)),
        ],
        out_specs=xspec,
        out_shape=jax.ShapeDtypeStruct((bsz, s, d), F32),
        scratch_shapes=[
            pltpu.VMEM((SWA_KV_HEADS, SWA_BLOCK + rows, LANES), BF16),
            pltpu.VMEM((SWA_KV_HEADS, SWA_BLOCK + rows, LANES), BF16),
            pltpu.VMEM((SWA_KV_HEADS, SWA_BLOCK + rows, LANES), BF16),
            pltpu.VMEM((SWA_KV_HEADS, SWA_BLOCK + rows, LANES), BF16),
            pltpu.VMEM((SWA_KV_HEADS, rows // SWA_BLOCK, d_q // SWA_KV_HEADS, LANES), BF16),
            pltpu.VMEM((2, d_q // SWA_KV_HEADS, 2 * SWA_BLOCK), BF16),
            pltpu.VMEM((rows, d_q), BF16),
        ],
        compiler_params=_compiler_params(("parallel", "arbitrary")),
        name="swa",
    )(x3, positions.reshape(bsz, s, 1), norm.reshape(1, d), w_qkv.astype(BF16),
      b_qkv.reshape(1, n_qkv), invf, sinks, w_out.astype(BF16), b_out.reshape(1, d))


def kernel(x, positions, l0_ffn1_norm, l0_ffn1_w_in, l0_ffn1_w_out, l0_mix_norm, l0_gla_w_in, l0_gla_w_gk_up, l0_gla_b_gk, l0_gla_o_norm, l0_gla_w_out, l0_ffn2_norm, l0_ffn2_w_in, l0_ffn2_w_out, l1_ffn1_norm, l1_ffn1_w_in, l1_ffn1_w_out, l1_mix_norm, l1_sgu_w_in, l1_sgu_ln_gain, l1_sgu_ln_bias, l1_sgu_w_s, l1_sgu_b_s, l1_sgu_w_out, l1_ffn2_norm, l1_ffn2_w_in, l1_ffn2_w_out, l2_ffn1_norm, l2_ffn1_w_in, l2_ffn1_w_out, l2_mix_norm, l2_swa_w_qkv, l2_swa_b_qkv, l2_swa_sinks, l2_swa_w_out, l2_swa_b_out, l2_ffn2_norm, l2_ffn2_w_in, l2_ffn2_w_out, l3_ffn1_norm, l3_ffn1_w_in, l3_ffn1_w_out, l3_mix_norm, l3_gla_w_in, l3_gla_w_gk_up, l3_gla_b_gk, l3_gla_o_norm, l3_gla_w_out, l3_ffn2_norm, l3_ffn2_w_in, l3_ffn2_w_out, final_norm):
    bsz, s, d = x.shape

    def ffn(t, norm, w_in, w_out, final=False):
        return _ffn(t.reshape(bsz * s, d), norm, w_in, w_out, final_norm, final=final).reshape(bsz, s, d)

    x = ffn(x, l0_ffn1_norm, l0_ffn1_w_in, l0_ffn1_w_out)
    x = _gla(x, l0_mix_norm, l0_gla_w_in, l0_gla_w_gk_up, l0_gla_b_gk, l0_gla_o_norm, l0_gla_w_out)
    x = ffn(x, l0_ffn2_norm, l0_ffn2_w_in, l0_ffn2_w_out)
    x = ffn(x, l1_ffn1_norm, l1_ffn1_w_in, l1_ffn1_w_out)
    x = _sgu(x.reshape(bsz * s, d), l1_mix_norm, l1_sgu_w_in, l1_sgu_ln_gain, l1_sgu_ln_bias,
             l1_sgu_w_s, l1_sgu_b_s, l1_sgu_w_out).reshape(bsz, s, d)
    x = ffn(x, l1_ffn2_norm, l1_ffn2_w_in, l1_ffn2_w_out)
    x = ffn(x, l2_ffn1_norm, l2_ffn1_w_in, l2_ffn1_w_out)
    x = _swa(x, positions, l2_mix_norm, l2_swa_w_qkv, l2_swa_b_qkv, l2_swa_sinks, l2_swa_w_out, l2_swa_b_out)
    x = ffn(x, l2_ffn2_norm, l2_ffn2_w_in, l2_ffn2_w_out)
    x = ffn(x, l3_ffn1_norm, l3_ffn1_w_in, l3_ffn1_w_out)
    x = _gla(x, l3_mix_norm, l3_gla_w_in, l3_gla_w_gk_up, l3_gla_b_gk, l3_gla_o_norm, l3_gla_w_out)
    x = ffn(x, l3_ffn2_norm, l3_ffn2_w_in, l3_ffn2_w_out, final=True)
    return x
```

```python
import functools

import jax
import jax.numpy as jnp
from jax import lax
from jax.experimental import pallas as pl
from jax.experimental.pallas import tpu as pltpu

F32 = jnp.float32
BF16 = jnp.bfloat16

NORM_EPS = 1e-5
LANES = 128
VMEM_LIMIT_BYTES = 56 * 1024 * 1024

GLA_HEADS = 4
GLA_HEAD_K = 128
GLA_HEAD_V = 256
GLA_DK = GLA_HEADS * GLA_HEAD_K
GLA_DV = GLA_HEADS * GLA_HEAD_V
GLA_GATE_RANK = 16
GLA_GATE_NORMALIZER = 16.0
GLA_CHUNK = 64
SGU_GROUPS = 8
SGU_CHUNK = 128
SWA_HEADS = 16
SWA_KV_HEADS = 2
SWA_HEAD_DIM = 64
SWA_GROUP = SWA_HEADS // SWA_KV_HEADS
SWA_BLOCK = 128
ROPE_DIM = SWA_HEAD_DIM // 4
ROPE_HALF = ROPE_DIM // 2
ROPE_THETA = 500000.0
MASKED_SCORE = -1e30
LOG2_E = 1.4426950408889634


def _dot(a, b):
    return jnp.dot(a, b, preferred_element_type=F32)


def _dot_nt(a, b):
    return lax.dot_general(a, b, (((1,), (1,)), ((), ())), preferred_element_type=F32)


def _dot_tn(a, b):
    return lax.dot_general(a, b, (((0,), (0,)), ((), ())), preferred_element_type=F32)


def _rms_norm(x, gain):
    return x * lax.rsqrt(jnp.mean(x * x, axis=-1, keepdims=True) + NORM_EPS) * gain


def _silu(x):
    return x * (1.0 / (1.0 + jnp.exp(-x)))


def _resident(shape):
    zeros = (0,) * len(shape)
    return pl.BlockSpec(shape, lambda *_: zeros, pipeline_mode=pl.Buffered(1))


def _compiler_params(semantics):
    return pltpu.CompilerParams(dimension_semantics=semantics,
                                vmem_limit_bytes=VMEM_LIMIT_BYTES)


FFN_ROWS = 1024
FFN_COLS = 256


def _ffn_body(x_ref, g_ref, wi_ref, wo_ref, fg_ref, o_ref, act_ref, *, d_ff, final):
    x = x_ref[...]
    xn = _rms_norm(x, g_ref[...]).astype(BF16)
    for c in range(d_ff // FFN_COLS):
        lo = c * FFN_COLS
        gate = _dot(xn, wi_ref[:, lo:lo + FFN_COLS])
        up = _dot(xn, wi_ref[:, d_ff + lo:d_ff + lo + FFN_COLS])
        act_ref[:, lo:lo + FFN_COLS] = (_silu(gate) * up).astype(BF16)
    y = x + 0.5 * _dot(act_ref[...], wo_ref[...])
    if final:
        y = _rms_norm(y, fg_ref[...])
    o_ref[...] = y


def _ffn(x2, norm, w_in, w_out, final_gain, *, final):
    t, d = x2.shape
    d_ff = w_out.shape[0]
    rows = min(FFN_ROWS, t)
    body = functools.partial(_ffn_body, d_ff=d_ff, final=final)
    return pl.pallas_call(
        body,
        grid=(t // rows,),
        in_specs=[
            pl.BlockSpec((rows, d), lambda i: (i, 0)),
            _resident((1, d)),
            _resident((d, 2 * d_ff)),
            _resident((d_ff, d)),
            _resident((1, d)),
        ],
        out_specs=pl.BlockSpec((rows, d), lambda i: (i, 0)),
        out_shape=jax.ShapeDtypeStruct((t, d), F32),
        scratch_shapes=[pltpu.VMEM((rows, d_ff), BF16)],
        compiler_params=_compiler_params(("parallel",)),
        name="ffn_final" if final else "ffn",
    )(x2, norm.reshape(1, d), w_in.astype(BF16), w_out.astype(BF16), final_gain.reshape(1, d))


GLA_ROWS = 512


def _gla_body(x_ref, g_ref, wq_ref, wgk_ref, wup_ref, bgk_ref, on_ref, wo_ref, o_ref,
              state_ref, qd_ref, ki_ref, ks_ref, v_ref, attn_ref, oacc_ref, *, rows):
    n_chunks = rows // GLA_CHUNK

    @pl.when(pl.program_id(1) == 0)
    def _():
        state_ref[...] = jnp.zeros_like(state_ref)

    x = x_ref[...]
    xn = _rms_norm(x, g_ref[...]).astype(BF16)
    q = _dot(xn, wq_ref[:, 0:GLA_DK]) * (GLA_HEAD_K ** -0.5)
    k = _dot(xn, wq_ref[:, GLA_DK:2 * GLA_DK])
    v_ref[...] = _dot(xn, wq_ref[:, 2 * GLA_DK:2 * GLA_DK + GLA_DV]).astype(BF16)
    gk_low = _dot(xn, wgk_ref[...])
    z = _dot(gk_low.astype(BF16), wup_ref[...]) + bgk_ref[...]
    log_a = (jnp.minimum(z, 0.0) - jnp.log1p(jnp.exp(-jnp.abs(z)))) * (1.0 / GLA_GATE_NORMALIZER)

    ri = lax.broadcasted_iota(jnp.int32, (GLA_CHUNK, GLA_CHUNK), 0)
    ci = lax.broadcasted_iota(jnp.int32, (GLA_CHUNK, GLA_CHUNK), 1)
    causal = ri >= ci
    tril = causal.astype(BF16)
    tril3 = jnp.concatenate([tril, tril, tril], axis=1)
    la_hi = log_a.astype(BF16)
    rest = log_a - la_hi.astype(F32)
    la_mid = rest.astype(BF16)
    la_lo = (rest - la_mid.astype(F32)).astype(BF16)
    decays = []
    for c in range(n_chunks):
        rs = slice(c * GLA_CHUNK, (c + 1) * GLA_CHUNK)
        b = _dot(tril3, jnp.concatenate([la_hi[rs], la_mid[rs], la_lo[rs]], axis=0))
        b_last = b[GLA_CHUNK - 1:GLA_CHUNK]
        qd_ref[rs] = (q[rs] * jnp.exp(b)).astype(BF16)
        ki_ref[rs] = (k[rs] * jnp.exp(-b)).astype(BF16)
        ks_ref[rs] = (k[rs] * jnp.exp(b_last - b)).astype(BF16)
        decays.append(jnp.exp(b_last))

    for c in range(n_chunks):
        rs = slice(c * GLA_CHUNK, (c + 1) * GLA_CHUNK)
        for h in range(GLA_HEADS):
            ks_ = slice(h * GLA_HEAD_K, (h + 1) * GLA_HEAD_K)
            attn = jnp.where(causal, _dot_nt(qd_ref[rs, ks_], ki_ref[rs, ks_]), 0.0)
            attn_ref[h, rs] = attn.astype(BF16)

    states = [state_ref[h] for h in range(GLA_HEADS)]
    for c in range(n_chunks):
        rs = slice(c * GLA_CHUNK, (c + 1) * GLA_CHUNK)
        for h in range(GLA_HEADS):
            ks_ = slice(h * GLA_HEAD_K, (h + 1) * GLA_HEAD_K)
            vs_ = slice(h * GLA_HEAD_V, (h + 1) * GLA_HEAD_V)
            v_h = v_ref[rs, vs_]
            oacc_ref[rs, vs_] = (_dot(attn_ref[h, rs], v_h)
                                 + _dot_nt(qd_ref[rs, ks_], states[h].astype(BF16)))
            states[h] = states[h] * decays[c][:, ks_] + _dot_tn(v_h, ks_ref[rs, ks_])
    for h in range(GLA_HEADS):
        state_ref[h] = states[h]

    r = _dot(xn, wq_ref[:, 2 * GLA_DK + GLA_DV:2 * GLA_DK + 2 * GLA_DV])
    gain = on_ref[...]
    for h in range(GLA_HEADS):
        vs_ = slice(h * GLA_HEAD_V, (h + 1) * GLA_HEAD_V)
        o_h = _rms_norm(oacc_ref[:, vs_], gain)
        oacc_ref[:, vs_] = o_h * _silu(r[:, vs_])
    o_ref[...] = x + _dot(oacc_ref[...].astype(BF16), wo_ref[...])


def _gla(x3, norm, w_in, w_gk_up, b_gk, o_norm, w_out):
    bsz, s, d = x3.shape
    rows = min(GLA_ROWS, s)
    n_qkvr = 2 * GLA_DK + 2 * GLA_DV
    w_qkvr = w_in[:, :n_qkvr].astype(BF16)
    w_gk = jnp.pad(w_in[:, n_qkvr:], ((0, 0), (0, LANES - GLA_GATE_RANK))).astype(BF16)
    w_up = jnp.pad(w_gk_up, ((0, LANES - GLA_GATE_RANK), (0, 0))).astype(BF16)
    body = functools.partial(_gla_body, rows=rows)
    xspec = pl.BlockSpec((None, rows, d), lambda b, i: (b, i, 0))
    return pl.pallas_call(
        body,
        grid=(bsz, s // rows),
        in_specs=[
            xspec,
            _resident((1, d)),
            _resident((d, n_qkvr)),
            _resident((d, LANES)),
            _resident((LANES, GLA_DK)),
            _resident((1, GLA_DK)),
            _resident((1, GLA_HEAD_V)),
            _resident((GLA_DV, d)),
        ],
        out_specs=xspec,
        out_shape=jax.ShapeDtypeStruct((bsz, s, d), F32),
        scratch_shapes=[
            pltpu.VMEM((GLA_HEADS, GLA_HEAD_V, GLA_HEAD_K), F32),
            pltpu.VMEM((rows, GLA_DK), BF16),
            pltpu.VMEM((rows, GLA_DK), BF16),
            pltpu.VMEM((rows, GLA_DK), BF16),
            pltpu.VMEM((rows, GLA_DV), BF16),
            pltpu.VMEM((GLA_HEADS, rows, GLA_CHUNK), BF16),
            pltpu.VMEM((rows, GLA_DV), F32),
        ],
        compiler_params=_compiler_params(("parallel", "arbitrary")),
        name="gla",
    )(x3, norm.reshape(1, d), w_qkvr, w_gk, w_up, b_gk.reshape(1, GLA_DK),
      o_norm.reshape(1, GLA_HEAD_V), w_out.astype(BF16))


SGU_ROWS = 512
SGU_COLS = 768


def _gelu(x):
    return 0.5 * x * (1.0 + lax.erf(x * (2.0 ** -0.5)))


def _sgu_body(x_ref, g_ref, wi_ref, lng_ref, lnb_ref, ws_ref, bs_ref, wo_ref, o_ref,
              v_ref, uv_ref, *, rows, half):
    x = x_ref[...]
    xn = _rms_norm(x, g_ref[...]).astype(BF16)
    n_col = half // SGU_COLS

    total = jnp.zeros((rows, 1), F32)
    for c in range(n_col):
        lo = c * SGU_COLS
        v_c = _gelu(_dot(xn, wi_ref[:, half + lo:half + lo + SGU_COLS]))
        v_ref[:, lo:lo + SGU_COLS] = v_c
        total = total + jnp.sum(v_c, axis=-1, keepdims=True)
    mu = total * (1.0 / half)
    sq = jnp.zeros((rows, 1), F32)
    for c in range(n_col):
        lo = c * SGU_COLS
        d_c = v_ref[:, lo:lo + SGU_COLS] - mu
        sq = sq + jnp.sum(d_c * d_c, axis=-1, keepdims=True)
    inv = lax.rsqrt(sq * (1.0 / half) + NORM_EPS)

    ri = lax.broadcasted_iota(jnp.int32, (SGU_CHUNK, SGU_CHUNK), 0)
    ci = lax.broadcasted_iota(jnp.int32, (SGU_CHUNK, SGU_CHUNK), 1)
    causal = ri >= ci
    group_dim = half // SGU_GROUPS
    n_chunks = rows // SGU_CHUNK
    for c in range(n_col):
        u_c = _gelu(_dot(xn, wi_ref[:, c * SGU_COLS:(c + 1) * SGU_COLS]))
        for g in range(c * SGU_COLS // group_dim, (c + 1) * SGU_COLS // group_dim):
            lo = g * group_dim
            cols = slice(lo, lo + group_dim)
            vn = ((v_ref[:, cols] - mu) * inv * lng_ref[:, cols] + lnb_ref[:, cols]).astype(BF16)
            w_g = jnp.where(causal, ws_ref[g], 0.0).astype(BF16)
            bias = bs_ref[:, g:g + 1]
            vn_wide = jnp.concatenate(
                [vn[j * SGU_CHUNK:(j + 1) * SGU_CHUNK] for j in range(n_chunks)], axis=1)
            mixed = _dot(w_g, vn_wide) + bias
            for j in range(n_chunks):
                rs = slice(j * SGU_CHUNK, (j + 1) * SGU_CHUNK)
                u_g = u_c[rs, lo - c * SGU_COLS:lo - c * SGU_COLS + group_dim]
                uv_ref[rs, cols] = (u_g * mixed[:, j * group_dim:(j + 1) * group_dim]).astype(BF16)
    o_ref[...] = x + _dot(uv_ref[...], wo_ref[...])


def _sgu(x2, norm, w_in, ln_gain, ln_bias, w_s, b_s, w_out):
    t, d = x2.shape
    half = w_out.shape[0]
    rows = min(SGU_ROWS, t)
    body = functools.partial(_sgu_body, rows=rows, half=half)
    return pl.pallas_call(
        body,
        grid=(t // rows,),
        in_specs=[
            pl.BlockSpec((rows, d), lambda i: (i, 0)),
            _resident((1, d)),
            _resident((d, 2 * half)),
            _resident((1, half)),
            _resident((1, half)),
            _resident((SGU_GROUPS, SGU_CHUNK, SGU_CHUNK)),
            _resident((SGU_CHUNK, SGU_GROUPS)),
            _resident((half, d)),
        ],
        out_specs=pl.BlockSpec((rows, d), lambda i: (i, 0)),
        out_shape=jax.ShapeDtypeStruct((t, d), F32),
        scratch_shapes=[
            pltpu.VMEM((rows, half), F32),
            pltpu.VMEM((rows, half), BF16),
        ],
        compiler_params=_compiler_params(("parallel",)),
        name="sgu",
    )(x2, norm.reshape(1, d), w_in.astype(BF16), ln_gain.reshape(1, half),
      ln_bias.reshape(1, half), w_s, b_s.T, w_out.astype(BF16))


SWA_ROWS = 512


def _swa_body(x_ref, pos_ref, g_ref, wqkv_ref, bqkv_ref, invf_ref, sinks_ref, wo_ref, bo_ref,
              o_ref, klo_ref, khi_ref, vlo_ref, vhi_ref, q_ref, p_ref, attn_ref, *, rows):
    d_q = SWA_HEADS * SWA_HEAD_DIM
    d_kv = SWA_KV_HEADS * SWA_HEAD_DIM
    n_blocks = rows // SWA_BLOCK
    tiles_per_kv = SWA_GROUP * SWA_HEAD_DIM // LANES
    step = pl.program_id(1)
    kv_refs = (klo_ref, khi_ref, vlo_ref, vhi_ref)

    @pl.when(step == 0)
    def _():
        for ref in kv_refs:
            ref[:, 0:SWA_BLOCK] = jnp.zeros((SWA_KV_HEADS, SWA_BLOCK, LANES), BF16)

    @pl.when(step > 0)
    def _():
        for ref in kv_refs:
            ref[:, 0:SWA_BLOCK] = ref[:, rows:rows + SWA_BLOCK]

    x = x_ref[...]
    xn = _rms_norm(x, g_ref[...]).astype(BF16)

    ang = invf_ref[...] * pos_ref[...].astype(F32)
    sel = (lax.broadcasted_iota(jnp.int32, (3 * ROPE_HALF, LANES), 0) % ROPE_HALF
           == lax.broadcasted_iota(jnp.int32, (3 * ROPE_HALF, LANES), 1) % ROPE_HALF).astype(F32)

    def spread(c):
        hi = c.astype(BF16).astype(F32)
        mid = (c - hi).astype(BF16).astype(F32)
        lo = (c - hi - mid).astype(BF16).astype(F32)
        return _dot_tn(jnp.concatenate([hi, mid, lo], axis=0), sel)

    lane = lax.broadcasted_iota(jnp.int32, (1, LANES), 1)
    head_lane = lane % SWA_HEAD_DIM
    sin_t = spread(jnp.sin(ang))
    cos_t = jnp.where(head_lane < ROPE_DIM, spread(jnp.cos(ang)), 1.0)
    sin_up = jnp.where((head_lane >= ROPE_HALF) & (head_lane < ROPE_DIM), sin_t, 0.0)
    sin_dn = jnp.where(head_lane < ROPE_HALF, -sin_t, 0.0)

    def rope(t):
        return (t * cos_t + pltpu.roll(t, ROPE_HALF, 1) * sin_up
                + pltpu.roll(t, LANES - ROPE_HALF, 1) * sin_dn)

    low_half = lane < SWA_HEAD_DIM
    kv = _dot(xn, wqkv_ref[:, d_q:d_q + 2 * d_kv]) + bqkv_ref[:, d_q:d_q + 2 * d_kv]
    cur = slice(SWA_BLOCK, SWA_BLOCK + rows)
    for t, lo_ref, hi_ref in ((rope(kv[:, :d_kv]), klo_ref, khi_ref), (kv[:, d_kv:], vlo_ref, vhi_ref)):
        swapped = pltpu.roll(t, SWA_HEAD_DIM, 1)
        lo_ref[0, cur] = jnp.where(low_half, t, 0.0).astype(BF16)
        hi_ref[0, cur] = jnp.where(low_half, 0.0, swapped).astype(BF16)
        lo_ref[1, cur] = jnp.where(low_half, swapped, 0.0).astype(BF16)
        hi_ref[1, cur] = jnp.where(low_half, 0.0, t).astype(BF16)

    q_cols = 2 * LANES
    for c in range(d_q // q_cols):
        q2 = _dot(xn, wqkv_ref[:, c * q_cols:(c + 1) * q_cols]) + bqkv_ref[:, c * q_cols:(c + 1) * q_cols]
        for t in range(q_cols // LANES):
            tile = c * (q_cols // LANES) + t
            kvh, slot = divmod(tile, tiles_per_kv)
            q_t = (rope(q2[:, t * LANES:(t + 1) * LANES]) * (SWA_HEAD_DIM ** -0.5 * LOG2_E)).astype(BF16)
            for j in range(n_blocks):
                q_ref[kvh, j, slot * SWA_BLOCK:(slot + 1) * SWA_BLOCK] = q_t[j * SWA_BLOCK:(j + 1) * SWA_BLOCK]

    qi = lax.broadcasted_iota(jnp.int32, (SWA_BLOCK, 2 * SWA_BLOCK), 0)
    kj = lax.broadcasted_iota(jnp.int32, (SWA_BLOCK, 2 * SWA_BLOCK), 1)
    delta = qi + SWA_BLOCK - kj
    in_window = (delta >= 0) & (delta < SWA_BLOCK)

    items = [(j, kvh) for j in range(n_blocks) for kvh in range(SWA_KV_HEADS)]

    def score_dots(j, kvh):
        band = slice(j * SWA_BLOCK, (j + 2) * SWA_BLOCK)
        qs = q_ref[kvh, j]
        return [_dot_nt(qs, k_ref[kvh, band]) for k_ref in (klo_ref, khi_ref)]

    s_next = score_dots(*items[0])
    for n, (j, kvh) in enumerate(items):
        rs = slice(j * SWA_BLOCK, (j + 1) * SWA_BLOCK)
        band = slice(j * SWA_BLOCK, (j + 2) * SWA_BLOCK)
        if j == 0:
            valid = in_window & (kj >= jnp.where(step > 0, 0, SWA_BLOCK))
        else:
            valid = in_window
        s_cur = s_next
        if n + 1 < len(items):
            s_next = score_dots(*items[n + 1])
        buf = n % 2
        inv = []
        for e in range(2):
            inv_e = []
            for slot in range(tiles_per_kv):
                ts = slice(slot * SWA_BLOCK, (slot + 1) * SWA_BLOCK)
                sink = sinks_ref[(kvh * tiles_per_kv + slot) * 2 + e] * LOG2_E
                scores = jnp.where(valid, s_cur[e][ts], MASKED_SCORE)
                m = jnp.maximum(jnp.max(scores, axis=-1, keepdims=True), sink)
                pr = jnp.exp2(scores - m)
                denom = jnp.sum(pr, axis=-1, keepdims=True) + jnp.exp2(sink - m)
                p_ref[buf, e, ts] = pr.astype(BF16)
                inv_e.append(1.0 / denom)
            inv.append(inv_e)
        out = _dot(p_ref[buf, 0], vlo_ref[kvh, band]) + _dot(p_ref[buf, 1], vhi_ref[kvh, band])
        for slot in range(tiles_per_kv):
            ts = slice(slot * SWA_BLOCK, (slot + 1) * SWA_BLOCK)
            tile = kvh * tiles_per_kv + slot
            scale = jnp.where(low_half, inv[0][slot], inv[1][slot])
            attn_ref[rs, tile * LANES:(tile + 1) * LANES] = (out[ts] * scale).astype(BF16)
    o_ref[...] = x + _dot(attn_ref[...], wo_ref[...]) + bo_ref[...]


def _swa(x3, positions, norm, w_qkv, b_qkv, sinks, w_out, b_out):
    bsz, s, d = x3.shape
    rows = min(SWA_ROWS, s)
    d_q = SWA_HEADS * SWA_HEAD_DIM
    d_kv = SWA_KV_HEADS * SWA_HEAD_DIM
    n_qkv = d_q + 2 * d_kv
    invf = (ROPE_THETA ** (-jnp.arange(0, ROPE_DIM, 2, dtype=F32) / ROPE_DIM)).reshape(ROPE_HALF, 1)
    body = functools.partial(_swa_body, rows=rows)
    xspec = pl.BlockSpec((None, rows, d), lambda b, i: (b, i, 0))
    return pl.pallas_call(
        body,
        grid=(bsz, s // rows),
        in_specs=[
            xspec,
            pl.BlockSpec((None, 1, rows), lambda b, i: (b, 0, i)),
            _resident((1, d)),
            _resident((d, n_qkv)),
            _resident((1, n_qkv)),
            _resident((ROPE_HALF, 1)),
            pl.BlockSpec(memory_space=pltpu.SMEM),
            _resident((d_q, d)),
            _resident((1, d)),
        ],
        out_specs=xspec,
        out_shape=jax.ShapeDtypeStruct((bsz, s, d), F32),
        scratch_shapes=[
            pltpu.VMEM((SWA_KV_HEADS, SWA_BLOCK + rows, LANES), BF16),
            pltpu.VMEM((SWA_KV_HEADS, SWA_BLOCK + rows, LANES), BF16),
            pltpu.VMEM((SWA_KV_HEADS, SWA_BLOCK + rows, LANES), BF16),
            pltpu.VMEM((SWA_KV_HEADS, SWA_BLOCK + rows, LANES), BF16),
            pltpu.VMEM((SWA_KV_HEADS, rows // SWA_BLOCK, d_q // SWA_KV_HEADS, LANES), BF16),
            pltpu.VMEM((2, 2, d_q // SWA_KV_HEADS, 2 * SWA_BLOCK), BF16),
            pltpu.VMEM((rows, d_q), BF16),
        ],
        compiler_params=_compiler_params(("parallel", "arbitrary")),
        name="swa",
    )(x3, positions.reshape(bsz, 1, s), norm.reshape(1, d), w_qkv.astype(BF16),
      b_qkv.reshape(1, n_qkv), invf, sinks, w_out.astype(BF16), b_out.reshape(1, d))


def kernel(x, positions, l0_ffn1_norm, l0_ffn1_w_in, l0_ffn1_w_out, l0_mix_norm, l0_gla_w_in, l0_gla_w_gk_up, l0_gla_b_gk, l0_gla_o_norm, l0_gla_w_out, l0_ffn2_norm, l0_ffn2_w_in, l0_ffn2_w_out, l1_ffn1_norm, l1_ffn1_w_in, l1_ffn1_w_out, l1_mix_norm, l1_sgu_w_in, l1_sgu_ln_gain, l1_sgu_ln_bias, l1_sgu_w_s, l1_sgu_b_s, l1_sgu_w_out, l1_ffn2_norm, l1_ffn2_w_in, l1_ffn2_w_out, l2_ffn1_norm, l2_ffn1_w_in, l2_ffn1_w_out, l2_mix_norm, l2_swa_w_qkv, l2_swa_b_qkv, l2_swa_sinks, l2_swa_w_out, l2_swa_b_out, l2_ffn2_norm, l2_ffn2_w_in, l2_ffn2_w_out, l3_ffn1_norm, l3_ffn1_w_in, l3_ffn1_w_out, l3_mix_norm, l3_gla_w_in, l3_gla_w_gk_up, l3_gla_b_gk, l3_gla_o_norm, l3_gla_w_out, l3_ffn2_norm, l3_ffn2_w_in, l3_ffn2_w_out, final_norm):
    bsz, s, d = x.shape

    def ffn(t, norm, w_in, w_out, final=False):
        return _ffn(t.reshape(bsz * s, d), norm, w_in, w_out, final_norm, final=final).reshape(bsz, s, d)

    x = ffn(x, l0_ffn1_norm, l0_ffn1_w_in, l0_ffn1_w_out)
    x = _gla(x, l0_mix_norm, l0_gla_w_in, l0_gla_w_gk_up, l0_gla_b_gk, l0_gla_o_norm, l0_gla_w_out)
    x = ffn(x, l0_ffn2_norm, l0_ffn2_w_in, l0_ffn2_w_out)
    x = ffn(x, l1_ffn1_norm, l1_ffn1_w_in, l1_ffn1_w_out)
    x = _sgu(x.reshape(bsz * s, d), l1_mix_norm, l1_sgu_w_in, l1_sgu_ln_gain, l1_sgu_ln_bias,
             l1_sgu_w_s, l1_sgu_b_s, l1_sgu_w_out).reshape(bsz, s, d)
    x = ffn(x, l1_ffn2_norm, l1_ffn2_w_in, l1_ffn2_w_out)
    x = ffn(x, l2_ffn1_norm, l2_ffn1_w_in, l2_ffn1_w_out)
    x = _swa(x, positions, l2_mix_norm, l2_swa_w_qkv, l2_swa_b_qkv, l2_swa_sinks, l2_swa_w_out, l2_swa_b_out)
    x = ffn(x, l2_ffn2_norm, l2_ffn2_w_in, l2_ffn2_w_out)
    x = ffn(x, l3_ffn1_norm, l3_ffn1_w_in, l3_ffn1_w_out)
    x = _gla(x, l3_mix_norm, l3_gla_w_in, l3_gla_w_gk_up, l3_gla_b_gk, l3_gla_o_norm, l3_gla_w_out)
    x = ffn(x, l3_ffn2_norm, l3_ffn2_w_in, l3_ffn2_w_out, final=True)
    return x
```

```python
import functools

import jax
import jax.numpy as jnp
from jax import lax
from jax.experimental import pallas as pl
from jax.experimental.pallas import tpu as pltpu

F32 = jnp.float32
BF16 = jnp.bfloat16

NORM_EPS = 1e-5
LANES = 128
VMEM_LIMIT_BYTES = 56 * 1024 * 1024

GLA_HEADS = 4
GLA_HEAD_K = 128
GLA_HEAD_V = 256
GLA_DK = GLA_HEADS * GLA_HEAD_K
GLA_DV = GLA_HEADS * GLA_HEAD_V
GLA_GATE_RANK = 16
GLA_GATE_NORMALIZER = 16.0
GLA_CHUNK = 64
SGU_GROUPS = 8
SGU_CHUNK = 128
SWA_HEADS = 16
SWA_KV_HEADS = 2
SWA_HEAD_DIM = 64
SWA_GROUP = SWA_HEADS // SWA_KV_HEADS
SWA_BLOCK = 128
ROPE_DIM = SWA_HEAD_DIM // 4
ROPE_HALF = ROPE_DIM // 2
ROPE_THETA = 500000.0
MASKED_SCORE = -1e30
LOG2_E = 1.4426950408889634


def _dot(a, b):
    return jnp.dot(a, b, preferred_element_type=F32)


def _dot_nt(a, b):
    return lax.dot_general(a, b, (((1,), (1,)), ((), ())), preferred_element_type=F32)


def _dot_tn(a, b):
    return lax.dot_general(a, b, (((0,), (0,)), ((), ())), preferred_element_type=F32)


def _rms_norm(x, gain):
    return x * lax.rsqrt(jnp.mean(x * x, axis=-1, keepdims=True) + NORM_EPS) * gain


def _silu(x):
    return x * (1.0 / (1.0 + jnp.exp(-x)))


def _resident(shape):
    zeros = (0,) * len(shape)
    return pl.BlockSpec(shape, lambda *_: zeros, pipeline_mode=pl.Buffered(1))


def _compiler_params(semantics):
    return pltpu.CompilerParams(dimension_semantics=semantics,
                                vmem_limit_bytes=VMEM_LIMIT_BYTES)


BF16_SUBLANES = 16


def _whole(w):
    return (w, [(0, w.shape[1], w.shape[1])])


def _cast_rows(n_rows, n_steps):
    for rb in range(BF16_SUBLANES, n_rows + 1, BF16_SUBLANES):
        if n_rows % rb == 0 and n_rows // rb <= n_steps:
            return rb
    raise ValueError("no aligned row block for %d rows in %d steps" % (n_rows, n_steps))


def _cast_blocks(in_refs, out_refs, col_plan):
    outs = iter(out_refs)
    for ref, cols in zip(in_refs, col_plan):
        for lo, hi, width in cols:
            blk = ref[:, lo:hi]
            if width > hi - lo:
                blk = jnp.concatenate([blk, jnp.zeros((blk.shape[0], width - (hi - lo)), F32)], axis=1)
            next(outs)[...] = blk.astype(BF16)


def _sub_block_call(body, *, grid, semantics, in_specs, out_spec, out_shape, scratch_shapes,
                    name, args, casts):
    n_steps = 1
    for g in grid:
        n_steps *= g

    def step_of(*idx):
        step = idx[0]
        for i, g in zip(idx[1:], grid[1:]):
            step = step * g + i
        return step

    cast_in_specs, cast_out_specs, cast_out_shapes, col_plan = [], [], [], []
    for arr, cols in casts:
        n_rows, n_cols = arr.shape
        rb = _cast_rows(n_rows, n_steps)
        n_blocks = n_rows // rb
        stride = n_steps // n_blocks

        def index_map(*idx, stride=stride, n_blocks=n_blocks):
            return (jnp.minimum(step_of(*idx) // stride, n_blocks - 1), 0)

        cast_in_specs.append(pl.BlockSpec((rb, n_cols), index_map))
        col_plan.append(cols)
        for _, _, width in cols:
            cast_out_specs.append(pl.BlockSpec((rb, width), index_map))
            cast_out_shapes.append(jax.ShapeDtypeStruct((n_rows, width), BF16))

    n_in, n_cast_in, n_cast_out = len(in_specs), len(casts), len(cast_out_specs)

    def body_with_casts(*refs):
        main_in = refs[:n_in]
        cast_in = refs[n_in:n_in + n_cast_in]
        out_ref = refs[n_in + n_cast_in]
        cast_out = refs[n_in + n_cast_in + 1:n_in + n_cast_in + 1 + n_cast_out]
        scratch = refs[n_in + n_cast_in + 1 + n_cast_out:]
        _cast_blocks(cast_in, cast_out, col_plan)
        body(*main_in, out_ref, *scratch)

    outs = pl.pallas_call(
        body_with_casts,
        grid=grid,
        in_specs=list(in_specs) + cast_in_specs,
        out_specs=[out_spec] + cast_out_specs,
        out_shape=[out_shape] + cast_out_shapes,
        scratch_shapes=scratch_shapes,
        compiler_params=_compiler_params(semantics),
        name=name,
    )(*args, *[arr for arr, _ in casts])
    return outs[0], list(outs[1:])


FFN_ROWS = 1024
FFN_COLS = 256


def _ffn_body(x_ref, g_ref, wi_ref, wo_ref, fg_ref, o_ref, act_ref, *, d_ff, final):
    x = x_ref[...]
    xn = _rms_norm(x, g_ref[...]).astype(BF16)
    for c in range(d_ff // FFN_COLS):
        lo = c * FFN_COLS
        gate = _dot(xn, wi_ref[:, lo:lo + FFN_COLS])
        up = _dot(xn, wi_ref[:, d_ff + lo:d_ff + lo + FFN_COLS])
        act_ref[:, lo:lo + FFN_COLS] = (_silu(gate) * up).astype(BF16)
    y = x + 0.5 * _dot(act_ref[...], wo_ref[...])
    if final:
        y = _rms_norm(y, fg_ref[...])
    o_ref[...] = y


def _ffn(x2, norm, w_in, w_out, final_gain, *, final, casts=()):
    t, d = x2.shape
    d_ff = w_out.shape[0]
    rows = min(FFN_ROWS, t)
    return _sub_block_call(
        functools.partial(_ffn_body, d_ff=d_ff, final=final),
        grid=(t // rows,),
        semantics=("arbitrary",),
        in_specs=[
            pl.BlockSpec((rows, d), lambda i: (i, 0)),
            _resident((1, d)),
            _resident((d, 2 * d_ff)),
            _resident((d_ff, d)),
            _resident((1, d)),
        ],
        out_spec=pl.BlockSpec((rows, d), lambda i: (i, 0)),
        out_shape=jax.ShapeDtypeStruct((t, d), F32),
        scratch_shapes=[pltpu.VMEM((rows, d_ff), BF16)],
        name="ffn_final" if final else "ffn",
        args=(x2, norm.reshape(1, d), w_in, w_out, final_gain.reshape(1, d)),
        casts=casts,
    )


GLA_ROWS = 512


def _gla_body(x_ref, g_ref, wq_ref, wgk_ref, wup_ref, bgk_ref, on_ref, wo_ref, o_ref,
              state_ref, qd_ref, ki_ref, ks_ref, v_ref, attn_ref, oacc_ref, *, rows):
    n_chunks = rows // GLA_CHUNK

    @pl.when(pl.program_id(1) == 0)
    def _():
        state_ref[...] = jnp.zeros_like(state_ref)

    x = x_ref[...]
    xn = _rms_norm(x, g_ref[...]).astype(BF16)
    q = _dot(xn, wq_ref[:, 0:GLA_DK]) * (GLA_HEAD_K ** -0.5)
    k = _dot(xn, wq_ref[:, GLA_DK:2 * GLA_DK])
    v_ref[...] = _dot(xn, wq_ref[:, 2 * GLA_DK:2 * GLA_DK + GLA_DV]).astype(BF16)
    gk_low = _dot(xn, wgk_ref[...])
    z = _dot(gk_low.astype(BF16), wup_ref[...]) + bgk_ref[...]
    log_a = (jnp.minimum(z, 0.0) - jnp.log1p(jnp.exp(-jnp.abs(z)))) * (1.0 / GLA_GATE_NORMALIZER)

    ri = lax.broadcasted_iota(jnp.int32, (GLA_CHUNK, GLA_CHUNK), 0)
    ci = lax.broadcasted_iota(jnp.int32, (GLA_CHUNK, GLA_CHUNK), 1)
    causal = ri >= ci
    tril = causal.astype(BF16)
    tril3 = jnp.concatenate([tril, tril, tril], axis=1)
    la_hi = log_a.astype(BF16)
    rest = log_a - la_hi.astype(F32)
    la_mid = rest.astype(BF16)
    la_lo = (rest - la_mid.astype(F32)).astype(BF16)
    decays = []
    for c in range(n_chunks):
        rs = slice(c * GLA_CHUNK, (c + 1) * GLA_CHUNK)
        b = _dot(tril3, jnp.concatenate([la_hi[rs], la_mid[rs], la_lo[rs]], axis=0))
        b_last = b[GLA_CHUNK - 1:GLA_CHUNK]
        qd_ref[rs] = (q[rs] * jnp.exp(b)).astype(BF16)
        ki_ref[rs] = (k[rs] * jnp.exp(-b)).astype(BF16)
        ks_ref[rs] = (k[rs] * jnp.exp(b_last - b)).astype(BF16)
        decays.append(jnp.exp(b_last))

    for c in range(n_chunks):
        rs = slice(c * GLA_CHUNK, (c + 1) * GLA_CHUNK)
        for h in range(GLA_HEADS):
            ks_ = slice(h * GLA_HEAD_K, (h + 1) * GLA_HEAD_K)
            attn = jnp.where(causal, _dot_nt(qd_ref[rs, ks_], ki_ref[rs, ks_]), 0.0)
            attn_ref[h, rs] = attn.astype(BF16)

    states = [state_ref[h] for h in range(GLA_HEADS)]
    for c in range(n_chunks):
        rs = slice(c * GLA_CHUNK, (c + 1) * GLA_CHUNK)
        for h in range(GLA_HEADS):
            ks_ = slice(h * GLA_HEAD_K, (h + 1) * GLA_HEAD_K)
            vs_ = slice(h * GLA_HEAD_V, (h + 1) * GLA_HEAD_V)
            v_h = v_ref[rs, vs_]
            oacc_ref[rs, vs_] = (_dot(attn_ref[h, rs], v_h)
                                 + _dot_nt(qd_ref[rs, ks_], states[h].astype(BF16)))
            states[h] = states[h] * decays[c][:, ks_] + _dot_tn(v_h, ks_ref[rs, ks_])
    for h in range(GLA_HEADS):
        state_ref[h] = states[h]

    r = _dot(xn, wq_ref[:, 2 * GLA_DK + GLA_DV:2 * GLA_DK + 2 * GLA_DV])
    gain = on_ref[...]
    for h in range(GLA_HEADS):
        vs_ = slice(h * GLA_HEAD_V, (h + 1) * GLA_HEAD_V)
        o_h = _rms_norm(oacc_ref[:, vs_], gain)
        oacc_ref[:, vs_] = o_h * _silu(r[:, vs_])
    o_ref[...] = x + _dot(oacc_ref[...].astype(BF16), wo_ref[...])


GLA_N_QKVR = 2 * GLA_DK + 2 * GLA_DV


def _gla_casts(w_in, w_out):
    return [(w_in, [(0, GLA_N_QKVR, GLA_N_QKVR), (GLA_N_QKVR, w_in.shape[1], LANES)]), _whole(w_out)]


def _gla(x3, norm, w_qkvr, w_gk, w_gk_up, b_gk, o_norm, w_out, *, casts=()):
    bsz, s, d = x3.shape
    rows = min(GLA_ROWS, s)
    w_up = jnp.pad(w_gk_up, ((0, LANES - GLA_GATE_RANK), (0, 0))).astype(BF16)
    xspec = pl.BlockSpec((None, rows, d), lambda b, i: (b, i, 0))
    return _sub_block_call(
        functools.partial(_gla_body, rows=rows),
        grid=(bsz, s // rows),
        semantics=("arbitrary", "arbitrary"),
        in_specs=[
            xspec,
            _resident((1, d)),
            _resident((d, GLA_N_QKVR)),
            _resident((d, LANES)),
            _resident((LANES, GLA_DK)),
            _resident((1, GLA_DK)),
            _resident((1, GLA_HEAD_V)),
            _resident((GLA_DV, d)),
        ],
        out_spec=xspec,
        out_shape=jax.ShapeDtypeStruct((bsz, s, d), F32),
        scratch_shapes=[
            pltpu.VMEM((GLA_HEADS, GLA_HEAD_V, GLA_HEAD_K), F32),
            pltpu.VMEM((rows, GLA_DK), BF16),
            pltpu.VMEM((rows, GLA_DK), BF16),
            pltpu.VMEM((rows, GLA_DK), BF16),
            pltpu.VMEM((rows, GLA_DV), BF16),
            pltpu.VMEM((GLA_HEADS, rows, GLA_CHUNK), BF16),
            pltpu.VMEM((rows, GLA_DV), F32),
        ],
        name="gla",
        args=(x3, norm.reshape(1, d), w_qkvr, w_gk, w_up, b_gk.reshape(1, GLA_DK),
              o_norm.reshape(1, GLA_HEAD_V), w_out),
        casts=casts,
    )


SGU_ROWS = 512
SGU_COLS = 768


def _gelu(x):
    return 0.5 * x * (1.0 + lax.erf(x * (2.0 ** -0.5)))


def _sgu_body(x_ref, g_ref, wi_ref, lng_ref, lnb_ref, ws_ref, bs_ref, wo_ref, o_ref,
              v_ref, uv_ref, *, rows, half):
    x = x_ref[...]
    xn = _rms_norm(x, g_ref[...]).astype(BF16)
    n_col = half // SGU_COLS

    total = jnp.zeros((rows, 1), F32)
    for c in range(n_col):
        lo = c * SGU_COLS
        v_c = _gelu(_dot(xn, wi_ref[:, half + lo:half + lo + SGU_COLS]))
        v_ref[:, lo:lo + SGU_COLS] = v_c
        total = total + jnp.sum(v_c, axis=-1, keepdims=True)
    mu = total * (1.0 / half)
    sq = jnp.zeros((rows, 1), F32)
    for c in range(n_col):
        lo = c * SGU_COLS
        d_c = v_ref[:, lo:lo + SGU_COLS] - mu
        sq = sq + jnp.sum(d_c * d_c, axis=-1, keepdims=True)
    inv = lax.rsqrt(sq * (1.0 / half) + NORM_EPS)

    ri = lax.broadcasted_iota(jnp.int32, (SGU_CHUNK, SGU_CHUNK), 0)
    ci = lax.broadcasted_iota(jnp.int32, (SGU_CHUNK, SGU_CHUNK), 1)
    causal = ri >= ci
    group_dim = half // SGU_GROUPS
    n_chunks = rows // SGU_CHUNK
    for c in range(n_col):
        u_c = _gelu(_dot(xn, wi_ref[:, c * SGU_COLS:(c + 1) * SGU_COLS]))
        for g in range(c * SGU_COLS // group_dim, (c + 1) * SGU_COLS // group_dim):
            lo = g * group_dim
            cols = slice(lo, lo + group_dim)
            vn = ((v_ref[:, cols] - mu) * inv * lng_ref[:, cols] + lnb_ref[:, cols]).astype(BF16)
            w_g = jnp.where(causal, ws_ref[g], 0.0).astype(BF16)
            bias = bs_ref[:, g:g + 1]
            vn_wide = jnp.concatenate(
                [vn[j * SGU_CHUNK:(j + 1) * SGU_CHUNK] for j in range(n_chunks)], axis=1)
            mixed = _dot(w_g, vn_wide) + bias
            for j in range(n_chunks):
                rs = slice(j * SGU_CHUNK, (j + 1) * SGU_CHUNK)
                u_g = u_c[rs, lo - c * SGU_COLS:lo - c * SGU_COLS + group_dim]
                uv_ref[rs, cols] = (u_g * mixed[:, j * group_dim:(j + 1) * group_dim]).astype(BF16)
    o_ref[...] = x + _dot(uv_ref[...], wo_ref[...])


def _sgu(x2, norm, w_in, ln_gain, ln_bias, w_s, b_s, w_out, *, casts=()):
    t, d = x2.shape
    half = w_out.shape[0]
    rows = min(SGU_ROWS, t)
    return _sub_block_call(
        functools.partial(_sgu_body, rows=rows, half=half),
        grid=(t // rows,),
        semantics=("arbitrary",),
        in_specs=[
            pl.BlockSpec((rows, d), lambda i: (i, 0)),
            _resident((1, d)),
            _resident((d, 2 * half)),
            _resident((1, half)),
            _resident((1, half)),
            _resident((SGU_GROUPS, SGU_CHUNK, SGU_CHUNK)),
            _resident((SGU_CHUNK, SGU_GROUPS)),
            _resident((half, d)),
        ],
        out_spec=pl.BlockSpec((rows, d), lambda i: (i, 0)),
        out_shape=jax.ShapeDtypeStruct((t, d), F32),
        scratch_shapes=[
            pltpu.VMEM((rows, half), F32),
            pltpu.VMEM((rows, half), BF16),
        ],
        name="sgu",
        args=(x2, norm.reshape(1, d), w_in, ln_gain.reshape(1, half), ln_bias.reshape(1, half),
              w_s, b_s.T, w_out),
        casts=casts,
    )


SWA_ROWS = 512


def _swa_body(x_ref, pos_ref, g_ref, wqkv_ref, bqkv_ref, invf_ref, sinks_ref, wo_ref, bo_ref,
              o_ref, klo_ref, khi_ref, vlo_ref, vhi_ref, q_ref, p_ref, attn_ref, *, rows):
    d_q = SWA_HEADS * SWA_HEAD_DIM
    d_kv = SWA_KV_HEADS * SWA_HEAD_DIM
    n_blocks = rows // SWA_BLOCK
    tiles_per_kv = SWA_GROUP * SWA_HEAD_DIM // LANES
    step = pl.program_id(1)
    kv_refs = (klo_ref, khi_ref, vlo_ref, vhi_ref)

    @pl.when(step == 0)
    def _():
        for ref in kv_refs:
            ref[:, 0:SWA_BLOCK] = jnp.zeros((SWA_KV_HEADS, SWA_BLOCK, LANES), BF16)

    @pl.when(step > 0)
    def _():
        for ref in kv_refs:
            ref[:, 0:SWA_BLOCK] = ref[:, rows:rows + SWA_BLOCK]

    x = x_ref[...]
    xn = _rms_norm(x, g_ref[...]).astype(BF16)

    ang = invf_ref[...] * pos_ref[...].astype(F32)
    sel = (lax.broadcasted_iota(jnp.int32, (3 * ROPE_HALF, LANES), 0) % ROPE_HALF
           == lax.broadcasted_iota(jnp.int32, (3 * ROPE_HALF, LANES), 1) % ROPE_HALF).astype(F32)

    def spread(c):
        hi = c.astype(BF16).astype(F32)
        mid = (c - hi).astype(BF16).astype(F32)
        lo = (c - hi - mid).astype(BF16).astype(F32)
        return _dot_tn(jnp.concatenate([hi, mid, lo], axis=0), sel)

    lane = lax.broadcasted_iota(jnp.int32, (1, LANES), 1)
    head_lane = lane % SWA_HEAD_DIM
    sin_t = spread(jnp.sin(ang))
    cos_t = jnp.where(head_lane < ROPE_DIM, spread(jnp.cos(ang)), 1.0)
    sin_up = jnp.where((head_lane >= ROPE_HALF) & (head_lane < ROPE_DIM), sin_t, 0.0)
    sin_dn = jnp.where(head_lane < ROPE_HALF, -sin_t, 0.0)

    def rope(t):
        return (t * cos_t + pltpu.roll(t, ROPE_HALF, 1) * sin_up
                + pltpu.roll(t, LANES - ROPE_HALF, 1) * sin_dn)

    low_half = lane < SWA_HEAD_DIM
    kv = _dot(xn, wqkv_ref[:, d_q:d_q + 2 * d_kv]) + bqkv_ref[:, d_q:d_q + 2 * d_kv]
    cur = slice(SWA_BLOCK, SWA_BLOCK + rows)
    for t, lo_ref, hi_ref in ((rope(kv[:, :d_kv]), klo_ref, khi_ref), (kv[:, d_kv:], vlo_ref, vhi_ref)):
        swapped = pltpu.roll(t, SWA_HEAD_DIM, 1)
        lo_ref[0, cur] = jnp.where(low_half, t, 0.0).astype(BF16)
        hi_ref[0, cur] = jnp.where(low_half, 0.0, swapped).astype(BF16)
        lo_ref[1, cur] = jnp.where(low_half, swapped, 0.0).astype(BF16)
        hi_ref[1, cur] = jnp.where(low_half, 0.0, t).astype(BF16)

    q_cols = 2 * LANES
    for c in range(d_q // q_cols):
        q2 = _dot(xn, wqkv_ref[:, c * q_cols:(c + 1) * q_cols]) + bqkv_ref[:, c * q_cols:(c + 1) * q_cols]
        for t in range(q_cols // LANES):
            tile = c * (q_cols // LANES) + t
            kvh, slot = divmod(tile, tiles_per_kv)
            q_t = (rope(q2[:, t * LANES:(t + 1) * LANES]) * (SWA_HEAD_DIM ** -0.5 * LOG2_E)).astype(BF16)
            for j in range(n_blocks):
                q_ref[kvh, j, slot * SWA_BLOCK:(slot + 1) * SWA_BLOCK] = q_t[j * SWA_BLOCK:(j + 1) * SWA_BLOCK]

    qi = lax.broadcasted_iota(jnp.int32, (SWA_BLOCK, 2 * SWA_BLOCK), 0)
    kj = lax.broadcasted_iota(jnp.int32, (SWA_BLOCK, 2 * SWA_BLOCK), 1)
    delta = qi + SWA_BLOCK - kj
    in_window = (delta >= 0) & (delta < SWA_BLOCK)

    items = [(j, kvh) for j in range(n_blocks) for kvh in range(SWA_KV_HEADS)]

    def score_dots(j, kvh):
        band = slice(j * SWA_BLOCK, (j + 2) * SWA_BLOCK)
        qs = q_ref[kvh, j]
        return [_dot_nt(qs, k_ref[kvh, band]) for k_ref in (klo_ref, khi_ref)]

    s_next = score_dots(*items[0])
    for n, (j, kvh) in enumerate(items):
        rs = slice(j * SWA_BLOCK, (j + 1) * SWA_BLOCK)
        band = slice(j * SWA_BLOCK, (j + 2) * SWA_BLOCK)
        if j == 0:
            valid = in_window & (kj >= jnp.where(step > 0, 0, SWA_BLOCK))
        else:
            valid = in_window
        s_cur = s_next
        if n + 1 < len(items):
            s_next = score_dots(*items[n + 1])
        buf = n % 2
        inv = []
        for e in range(2):
            inv_e = []
            for slot in range(tiles_per_kv):
                ts = slice(slot * SWA_BLOCK, (slot + 1) * SWA_BLOCK)
                sink = sinks_ref[(kvh * tiles_per_kv + slot) * 2 + e] * LOG2_E
                scores = jnp.where(valid, s_cur[e][ts], MASKED_SCORE)
                m = jnp.maximum(jnp.max(scores, axis=-1, keepdims=True), sink)
                pr = jnp.exp2(scores - m)
                denom = jnp.sum(pr, axis=-1, keepdims=True) + jnp.exp2(sink - m)
                p_ref[buf, e, ts] = pr.astype(BF16)
                inv_e.append(1.0 / denom)
            inv.append(inv_e)
        out = _dot(p_ref[buf, 0], vlo_ref[kvh, band]) + _dot(p_ref[buf, 1], vhi_ref[kvh, band])
        for slot in range(tiles_per_kv):
            ts = slice(slot * SWA_BLOCK, (slot + 1) * SWA_BLOCK)
            tile = kvh * tiles_per_kv + slot
            scale = jnp.where(low_half, inv[0][slot], inv[1][slot])
            attn_ref[rs, tile * LANES:(tile + 1) * LANES] = (out[ts] * scale).astype(BF16)
    o_ref[...] = x + _dot(attn_ref[...], wo_ref[...]) + bo_ref[...]


def _swa(x3, positions, norm, w_qkv, b_qkv, sinks, w_out, b_out, *, casts=()):
    bsz, s, d = x3.shape
    rows = min(SWA_ROWS, s)
    d_q = SWA_HEADS * SWA_HEAD_DIM
    d_kv = SWA_KV_HEADS * SWA_HEAD_DIM
    n_qkv = d_q + 2 * d_kv
    invf = (ROPE_THETA ** (-jnp.arange(0, ROPE_DIM, 2, dtype=F32) / ROPE_DIM)).reshape(ROPE_HALF, 1)
    xspec = pl.BlockSpec((None, rows, d), lambda b, i: (b, i, 0))
    kv_buffer = pltpu.VMEM((SWA_KV_HEADS, SWA_BLOCK + rows, LANES), BF16)
    return _sub_block_call(
        functools.partial(_swa_body, rows=rows),
        grid=(bsz, s // rows),
        semantics=("arbitrary", "arbitrary"),
        in_specs=[
            xspec,
            pl.BlockSpec((None, 1, rows), lambda b, i: (b, 0, i)),
            _resident((1, d)),
            _resident((d, n_qkv)),
            _resident((1, n_qkv)),
            _resident((ROPE_HALF, 1)),
            pl.BlockSpec(memory_space=pltpu.SMEM),
            _resident((d_q, d)),
            _resident((1, d)),
        ],
        out_spec=xspec,
        out_shape=jax.ShapeDtypeStruct((bsz, s, d), F32),
        scratch_shapes=[
            kv_buffer, kv_buffer, kv_buffer, kv_buffer,
            pltpu.VMEM((SWA_KV_HEADS, rows // SWA_BLOCK, d_q // SWA_KV_HEADS, LANES), BF16),
            pltpu.VMEM((2, 2, d_q // SWA_KV_HEADS, 2 * SWA_BLOCK), BF16),
            pltpu.VMEM((rows, d_q), BF16),
        ],
        name="swa",
        args=(x3, positions.reshape(bsz, 1, s), norm.reshape(1, d), w_qkv, b_qkv.reshape(1, n_qkv),
              invf, sinks, w_out, b_out.reshape(1, d)),
        casts=casts,
    )


def kernel(x, positions, l0_ffn1_norm, l0_ffn1_w_in, l0_ffn1_w_out, l0_mix_norm, l0_gla_w_in, l0_gla_w_gk_up, l0_gla_b_gk, l0_gla_o_norm, l0_gla_w_out, l0_ffn2_norm, l0_ffn2_w_in, l0_ffn2_w_out, l1_ffn1_norm, l1_ffn1_w_in, l1_ffn1_w_out, l1_mix_norm, l1_sgu_w_in, l1_sgu_ln_gain, l1_sgu_ln_bias, l1_sgu_w_s, l1_sgu_b_s, l1_sgu_w_out, l1_ffn2_norm, l1_ffn2_w_in, l1_ffn2_w_out, l2_ffn1_norm, l2_ffn1_w_in, l2_ffn1_w_out, l2_mix_norm, l2_swa_w_qkv, l2_swa_b_qkv, l2_swa_sinks, l2_swa_w_out, l2_swa_b_out, l2_ffn2_norm, l2_ffn2_w_in, l2_ffn2_w_out, l3_ffn1_norm, l3_ffn1_w_in, l3_ffn1_w_out, l3_mix_norm, l3_gla_w_in, l3_gla_w_gk_up, l3_gla_b_gk, l3_gla_o_norm, l3_gla_w_out, l3_ffn2_norm, l3_ffn2_w_in, l3_ffn2_w_out, final_norm):
    bsz, s, d = x.shape

    def ffn(t, norm, w_bf16, casts, final=False):
        y, nxt = _ffn(t.reshape(bsz * s, d), norm, *w_bf16, final_norm, final=final, casts=casts)
        return y.reshape(bsz, s, d), nxt

    def ffn_casts(w_in, w_out):
        return [_whole(w_in), _whole(w_out)]

    w = [l0_ffn1_w_in.astype(BF16), l0_ffn1_w_out.astype(BF16)]
    x, w = ffn(x, l0_ffn1_norm, w, _gla_casts(l0_gla_w_in, l0_gla_w_out))
    x, w = _gla(x, l0_mix_norm, w[0], w[1], l0_gla_w_gk_up, l0_gla_b_gk, l0_gla_o_norm, w[2],
                casts=ffn_casts(l0_ffn2_w_in, l0_ffn2_w_out))
    x, w = ffn(x, l0_ffn2_norm, w, ffn_casts(l1_ffn1_w_in, l1_ffn1_w_out))
    x, w = ffn(x, l1_ffn1_norm, w, ffn_casts(l1_sgu_w_in, l1_sgu_w_out))
    x, w = _sgu(x.reshape(bsz * s, d), l1_mix_norm, w[0], l1_sgu_ln_gain, l1_sgu_ln_bias,
                l1_sgu_w_s, l1_sgu_b_s, w[1], casts=ffn_casts(l1_ffn2_w_in, l1_ffn2_w_out))
    x = x.reshape(bsz, s, d)
    x, w = ffn(x, l1_ffn2_norm, w, ffn_casts(l2_ffn1_w_in, l2_ffn1_w_out))
    x, w = ffn(x, l2_ffn1_norm, w, ffn_casts(l2_swa_w_qkv, l2_swa_w_out))
    x, w = _swa(x, positions, l2_mix_norm, w[0], l2_swa_b_qkv, l2_swa_sinks, w[1], l2_swa_b_out,
                casts=ffn_casts(l2_ffn2_w_in, l2_ffn2_w_out))
    x, w = ffn(x, l2_ffn2_norm, w, ffn_casts(l3_ffn1_w_in, l3_ffn1_w_out))
    x, w = ffn(x, l3_ffn1_norm, w, _gla_casts(l3_gla_w_in, l3_gla_w_out))
    x, w = _gla(x, l3_mix_norm, w[0], w[1], l3_gla_w_gk_up, l3_gla_b_gk, l3_gla_o_norm, w[2],
                casts=ffn_casts(l3_ffn2_w_in, l3_ffn2_w_out))
    x, _ = ffn(x, l3_ffn2_norm, w, [], final=True)
    return x
```

```python
import functools

import jax
import jax.numpy as jnp
from jax import lax
from jax.experimental import pallas as pl
from jax.experimental.pallas import tpu as pltpu

F32 = jnp.float32
BF16 = jnp.bfloat16

NORM_EPS = 1e-5
LANES = 128
VMEM_LIMIT_BYTES = 56 * 1024 * 1024

GLA_HEADS = 4
GLA_HEAD_K = 128
GLA_HEAD_V = 256
GLA_DK = GLA_HEADS * GLA_HEAD_K
GLA_DV = GLA_HEADS * GLA_HEAD_V
GLA_GATE_RANK = 16
GLA_GATE_NORMALIZER = 16.0
GLA_CHUNK = 64
SGU_GROUPS = 8
SGU_CHUNK = 128
SWA_HEADS = 16
SWA_KV_HEADS = 2
SWA_HEAD_DIM = 64
SWA_GROUP = SWA_HEADS // SWA_KV_HEADS
SWA_BLOCK = 128
ROPE_DIM = SWA_HEAD_DIM // 4
ROPE_HALF = ROPE_DIM // 2
ROPE_THETA = 500000.0
MASKED_SCORE = -1e30
LOG2_E = 1.4426950408889634


def _dot(a, b):
    return jnp.dot(a, b, preferred_element_type=F32)


def _dot_nt(a, b):
    return lax.dot_general(a, b, (((1,), (1,)), ((), ())), preferred_element_type=F32)


def _dot_tn(a, b):
    return lax.dot_general(a, b, (((0,), (0,)), ((), ())), preferred_element_type=F32)


def _rms_norm(x, gain):
    return x * lax.rsqrt(jnp.mean(x * x, axis=-1, keepdims=True) + NORM_EPS) * gain


def _silu(x):
    return x * (1.0 / (1.0 + jnp.exp(-x)))


def _resident(shape):
    zeros = (0,) * len(shape)
    return pl.BlockSpec(shape, lambda *_: zeros, pipeline_mode=pl.Buffered(1))


def _compiler_params(semantics):
    return pltpu.CompilerParams(dimension_semantics=semantics,
                                vmem_limit_bytes=VMEM_LIMIT_BYTES)


BF16_SUBLANES = 16


def _whole(w):
    return (w, [(0, w.shape[1], w.shape[1])])


def _cast_rows(n_rows, n_steps):
    for rb in range(BF16_SUBLANES, n_rows + 1, BF16_SUBLANES):
        if n_rows % rb == 0 and n_rows // rb <= n_steps:
            return rb
    raise ValueError("no aligned row block for %d rows in %d steps" % (n_rows, n_steps))


def _cast_blocks(in_refs, out_refs, col_plan):
    outs = iter(out_refs)
    for ref, cols in zip(in_refs, col_plan):
        for lo, hi, width in cols:
            blk = ref[:, lo:hi]
            if width > hi - lo:
                blk = jnp.concatenate([blk, jnp.zeros((blk.shape[0], width - (hi - lo)), F32)], axis=1)
            next(outs)[...] = blk.astype(BF16)


def _sub_block_call(body, *, grid, semantics, in_specs, out_spec, out_shape, scratch_shapes,
                    name, args, casts):
    n_steps = 1
    for g in grid:
        n_steps *= g

    def step_of(*idx):
        step = idx[0]
        for i, g in zip(idx[1:], grid[1:]):
            step = step * g + i
        return step

    cast_in_specs, cast_out_specs, cast_out_shapes, col_plan = [], [], [], []
    for arr, cols in casts:
        n_rows, n_cols = arr.shape
        rb = _cast_rows(n_rows, n_steps)
        n_blocks = n_rows // rb
        stride = n_steps // n_blocks

        def index_map(*idx, stride=stride, n_blocks=n_blocks):
            return (jnp.minimum(step_of(*idx) // stride, n_blocks - 1), 0)

        cast_in_specs.append(pl.BlockSpec((rb, n_cols), index_map))
        col_plan.append(cols)
        for _, _, width in cols:
            cast_out_specs.append(pl.BlockSpec((rb, width), index_map))
            cast_out_shapes.append(jax.ShapeDtypeStruct((n_rows, width), BF16))

    n_in, n_cast_in, n_cast_out = len(in_specs), len(casts), len(cast_out_specs)

    def body_with_casts(*refs):
        main_in = refs[:n_in]
        cast_in = refs[n_in:n_in + n_cast_in]
        out_ref = refs[n_in + n_cast_in]
        cast_out = refs[n_in + n_cast_in + 1:n_in + n_cast_in + 1 + n_cast_out]
        scratch = refs[n_in + n_cast_in + 1 + n_cast_out:]
        _cast_blocks(cast_in, cast_out, col_plan)
        body(*main_in, out_ref, *scratch)

    outs = pl.pallas_call(
        body_with_casts,
        grid=grid,
        in_specs=list(in_specs) + cast_in_specs,
        out_specs=[out_spec] + cast_out_specs,
        out_shape=[out_shape] + cast_out_shapes,
        scratch_shapes=scratch_shapes,
        compiler_params=_compiler_params(semantics),
        name=name,
    )(*args, *[arr for arr, _ in casts])
    return outs[0], list(outs[1:])


FFN_ROWS = 1024
FFN_COLS = 256


def _ffn_body(x_ref, g_ref, wi_ref, wo_ref, fg_ref, o_ref, act_ref, *, d_ff, final):
    x = x_ref[...]
    xn = _rms_norm(x, g_ref[...]).astype(BF16)
    for c in range(d_ff // FFN_COLS):
        lo = c * FFN_COLS
        gate = _dot(xn, wi_ref[:, lo:lo + FFN_COLS])
        up = _dot(xn, wi_ref[:, d_ff + lo:d_ff + lo + FFN_COLS])
        act_ref[:, lo:lo + FFN_COLS] = (_silu(gate) * up).astype(BF16)
    y = x + 0.5 * _dot(act_ref[...], wo_ref[...])
    if final:
        y = _rms_norm(y, fg_ref[...])
    o_ref[...] = y


def _ffn(x2, norm, w_in, w_out, final_gain, *, final, casts=()):
    t, d = x2.shape
    d_ff = w_out.shape[0]
    rows = min(FFN_ROWS, t)
    return _sub_block_call(
        functools.partial(_ffn_body, d_ff=d_ff, final=final),
        grid=(t // rows,),
        semantics=("arbitrary",),
        in_specs=[
            pl.BlockSpec((rows, d), lambda i: (i, 0)),
            _resident((1, d)),
            _resident((d, 2 * d_ff)),
            _resident((d_ff, d)),
            _resident((1, d)),
        ],
        out_spec=pl.BlockSpec((rows, d), lambda i: (i, 0)),
        out_shape=jax.ShapeDtypeStruct((t, d), F32),
        scratch_shapes=[pltpu.VMEM((rows, d_ff), BF16)],
        name="ffn_final" if final else "ffn",
        args=(x2, norm.reshape(1, d), w_in, w_out, final_gain.reshape(1, d)),
        casts=casts,
    )


GLA_ROWS = 512


def _gla_body(x_ref, g_ref, wq_ref, wgk_ref, wup_ref, bgk_ref, on_ref, wo_ref, o_ref,
              state_ref, qd_ref, ki_ref, ks_ref, v_ref, attn_ref, oacc_ref, gate_ref, *, rows):
    n_chunks = rows // GLA_CHUNK

    @pl.when(pl.program_id(1) == 0)
    def _():
        state_ref[...] = jnp.zeros_like(state_ref)

    x = x_ref[...]
    xn = _rms_norm(x, g_ref[...]).astype(BF16)
    gk_low = _dot(xn, wgk_ref[...])
    z = _dot(gk_low.astype(BF16), wup_ref[...]) + bgk_ref[...]
    v_ref[...] = _dot(xn, wq_ref[:, 2 * GLA_DK:2 * GLA_DK + GLA_DV]).astype(BF16)
    log_a = (jnp.minimum(z, 0.0) - jnp.log1p(jnp.exp(-jnp.abs(z)))) * (1.0 / GLA_GATE_NORMALIZER)
    q = _dot(xn, wq_ref[:, 0:GLA_DK]) * (GLA_HEAD_K ** -0.5)
    k = _dot(xn, wq_ref[:, GLA_DK:2 * GLA_DK])

    ri = lax.broadcasted_iota(jnp.int32, (GLA_CHUNK, GLA_CHUNK), 0)
    ci = lax.broadcasted_iota(jnp.int32, (GLA_CHUNK, GLA_CHUNK), 1)
    causal = ri >= ci
    tril = causal.astype(BF16)
    tril3 = jnp.concatenate([tril, tril, tril], axis=1)
    la_hi = log_a.astype(BF16)
    rest = log_a - la_hi.astype(F32)
    la_mid = rest.astype(BF16)
    la_lo = (rest - la_mid.astype(F32)).astype(BF16)
    gate_ref[...] = _silu(_dot(xn, wq_ref[:, 2 * GLA_DK + GLA_DV:2 * GLA_DK + 2 * GLA_DV]))
    decays = []
    for c in range(n_chunks):
        rs = slice(c * GLA_CHUNK, (c + 1) * GLA_CHUNK)
        b = _dot(tril3, jnp.concatenate([la_hi[rs], la_mid[rs], la_lo[rs]], axis=0))
        b_last = b[GLA_CHUNK - 1:GLA_CHUNK]
        qd_ref[rs] = (q[rs] * jnp.exp(b)).astype(BF16)
        ki_ref[rs] = (k[rs] * jnp.exp(-b)).astype(BF16)
        ks_ref[rs] = (k[rs] * jnp.exp(b_last - b)).astype(BF16)
        decays.append(jnp.exp(b_last))

    for c in range(n_chunks):
        rs = slice(c * GLA_CHUNK, (c + 1) * GLA_CHUNK)
        for h in range(GLA_HEADS):
            ks_ = slice(h * GLA_HEAD_K, (h + 1) * GLA_HEAD_K)
            attn = jnp.where(causal, _dot_nt(qd_ref[rs, ks_], ki_ref[rs, ks_]), 0.0)
            attn_ref[h, rs] = attn.astype(BF16)

    states = [state_ref[h] for h in range(GLA_HEADS)]
    for c in range(n_chunks):
        rs = slice(c * GLA_CHUNK, (c + 1) * GLA_CHUNK)
        for h in range(GLA_HEADS):
            ks_ = slice(h * GLA_HEAD_K, (h + 1) * GLA_HEAD_K)
            vs_ = slice(h * GLA_HEAD_V, (h + 1) * GLA_HEAD_V)
            v_h = v_ref[rs, vs_]
            oacc_ref[rs, vs_] = (_dot(attn_ref[h, rs], v_h)
                                 + _dot_nt(qd_ref[rs, ks_], states[h].astype(BF16)))
            states[h] = states[h] * decays[c][:, ks_] + _dot_tn(v_h, ks_ref[rs, ks_])
    for h in range(GLA_HEADS):
        state_ref[h] = states[h]

    gain = on_ref[...]
    y = x
    for h in range(GLA_HEADS):
        vs_ = slice(h * GLA_HEAD_V, (h + 1) * GLA_HEAD_V)
        o_h = _rms_norm(oacc_ref[:, vs_], gain) * gate_ref[:, vs_]
        y = y + _dot(o_h.astype(BF16), wo_ref[vs_, :])
    o_ref[...] = y


GLA_N_QKVR = 2 * GLA_DK + 2 * GLA_DV


def _gla_casts(w_in, w_out):
    return [(w_in, [(0, GLA_N_QKVR, GLA_N_QKVR), (GLA_N_QKVR, w_in.shape[1], LANES)]), _whole(w_out)]


def _gla(x3, norm, w_qkvr, w_gk, w_gk_up, b_gk, o_norm, w_out, *, casts=()):
    bsz, s, d = x3.shape
    rows = min(GLA_ROWS, s)
    w_up = jnp.pad(w_gk_up, ((0, LANES - GLA_GATE_RANK), (0, 0))).astype(BF16)
    xspec = pl.BlockSpec((None, rows, d), lambda b, i: (b, i, 0))
    return _sub_block_call(
        functools.partial(_gla_body, rows=rows),
        grid=(bsz, s // rows),
        semantics=("arbitrary", "arbitrary"),
        in_specs=[
            xspec,
            _resident((1, d)),
            _resident((d, GLA_N_QKVR)),
            _resident((d, LANES)),
            _resident((LANES, GLA_DK)),
            _resident((1, GLA_DK)),
            _resident((1, GLA_HEAD_V)),
            _resident((GLA_DV, d)),
        ],
        out_spec=xspec,
        out_shape=jax.ShapeDtypeStruct((bsz, s, d), F32),
        scratch_shapes=[
            pltpu.VMEM((GLA_HEADS, GLA_HEAD_V, GLA_HEAD_K), F32),
            pltpu.VMEM((rows, GLA_DK), BF16),
            pltpu.VMEM((rows, GLA_DK), BF16),
            pltpu.VMEM((rows, GLA_DK), BF16),
            pltpu.VMEM((rows, GLA_DV), BF16),
            pltpu.VMEM((GLA_HEADS, rows, GLA_CHUNK), BF16),
            pltpu.VMEM((rows, GLA_DV), F32),
            pltpu.VMEM((rows, GLA_DV), F32),
        ],
        name="gla",
        args=(x3, norm.reshape(1, d), w_qkvr, w_gk, w_up, b_gk.reshape(1, GLA_DK),
              o_norm.reshape(1, GLA_HEAD_V), w_out),
        casts=casts,
    )


SGU_ROWS = 512
SGU_COLS = 768


def _gelu(x):
    return 0.5 * x * (1.0 + lax.erf(x * (2.0 ** -0.5)))


def _sgu_body(x_ref, g_ref, wi_ref, lng_ref, lnb_ref, ws_ref, bs_ref, wo_ref, o_ref,
              v_ref, uv_ref, *, rows, half):
    x = x_ref[...]
    xn = _rms_norm(x, g_ref[...]).astype(BF16)
    n_col = half // SGU_COLS

    total = jnp.zeros((rows, 1), F32)
    for c in range(n_col):
        lo = c * SGU_COLS
        v_c = _gelu(_dot(xn, wi_ref[:, half + lo:half + lo + SGU_COLS]))
        v_ref[:, lo:lo + SGU_COLS] = v_c
        total = total + jnp.sum(v_c, axis=-1, keepdims=True)
    mu = total * (1.0 / half)
    sq = jnp.zeros((rows, 1), F32)
    for c in range(n_col):
        lo = c * SGU_COLS
        d_c = v_ref[:, lo:lo + SGU_COLS] - mu
        sq = sq + jnp.sum(d_c * d_c, axis=-1, keepdims=True)
    inv = lax.rsqrt(sq * (1.0 / half) + NORM_EPS)

    ri = lax.broadcasted_iota(jnp.int32, (SGU_CHUNK, SGU_CHUNK), 0)
    ci = lax.broadcasted_iota(jnp.int32, (SGU_CHUNK, SGU_CHUNK), 1)
    causal = ri >= ci
    group_dim = half // SGU_GROUPS
    n_chunks = rows // SGU_CHUNK
    for c in range(n_col):
        u_c = _gelu(_dot(xn, wi_ref[:, c * SGU_COLS:(c + 1) * SGU_COLS]))
        for g in range(c * SGU_COLS // group_dim, (c + 1) * SGU_COLS // group_dim):
            lo = g * group_dim
            cols = slice(lo, lo + group_dim)
            vn = ((v_ref[:, cols] - mu) * inv * lng_ref[:, cols] + lnb_ref[:, cols]).astype(BF16)
            w_g = jnp.where(causal, ws_ref[g], 0.0).astype(BF16)
            bias = bs_ref[:, g:g + 1]
            vn_wide = jnp.concatenate(
                [vn[j * SGU_CHUNK:(j + 1) * SGU_CHUNK] for j in range(n_chunks)], axis=1)
            mixed = _dot(w_g, vn_wide) + bias
            for j in range(n_chunks):
                rs = slice(j * SGU_CHUNK, (j + 1) * SGU_CHUNK)
                u_g = u_c[rs, lo - c * SGU_COLS:lo - c * SGU_COLS + group_dim]
                uv_ref[rs, cols] = (u_g * mixed[:, j * group_dim:(j + 1) * group_dim]).astype(BF16)
    o_ref[...] = x + _dot(uv_ref[...], wo_ref[...])


def _sgu(x2, norm, w_in, ln_gain, ln_bias, w_s, b_s, w_out, *, casts=()):
    t, d = x2.shape
    half = w_out.shape[0]
    rows = min(SGU_ROWS, t)
    return _sub_block_call(
        functools.partial(_sgu_body, rows=rows, half=half),
        grid=(t // rows,),
        semantics=("arbitrary",),
        in_specs=[
            pl.BlockSpec((rows, d), lambda i: (i, 0)),
            _resident((1, d)),
            _resident((d, 2 * half)),
            _resident((1, half)),
            _resident((1, half)),
            _resident((SGU_GROUPS, SGU_CHUNK, SGU_CHUNK)),
            _resident((SGU_CHUNK, SGU_GROUPS)),
            _resident((half, d)),
        ],
        out_spec=pl.BlockSpec((rows, d), lambda i: (i, 0)),
        out_shape=jax.ShapeDtypeStruct((t, d), F32),
        scratch_shapes=[
            pltpu.VMEM((rows, half), F32),
            pltpu.VMEM((rows, half), BF16),
        ],
        name="sgu",
        args=(x2, norm.reshape(1, d), w_in, ln_gain.reshape(1, half), ln_bias.reshape(1, half),
              w_s, b_s.T, w_out),
        casts=casts,
    )


SWA_ROWS = 512


def _swa_body(x_ref, pos_ref, g_ref, wqkv_ref, bqkv_ref, invf_ref, sinks_ref, wo_ref, bo_ref,
              o_ref, klo_ref, khi_ref, vlo_ref, vhi_ref, q_ref, p_ref, attn_ref, *, rows):
    d_q = SWA_HEADS * SWA_HEAD_DIM
    d_kv = SWA_KV_HEADS * SWA_HEAD_DIM
    n_blocks = rows // SWA_BLOCK
    tiles_per_kv = SWA_GROUP * SWA_HEAD_DIM // LANES
    step = pl.program_id(1)
    kv_refs = (klo_ref, khi_ref, vlo_ref, vhi_ref)

    @pl.when(step == 0)
    def _():
        for ref in kv_refs:
            ref[:, 0:SWA_BLOCK] = jnp.zeros((SWA_KV_HEADS, SWA_BLOCK, ref.shape[-1]), BF16)

    @pl.when(step > 0)
    def _():
        for ref in kv_refs:
            ref[:, 0:SWA_BLOCK] = ref[:, rows:rows + SWA_BLOCK]

    x = x_ref[...]
    xn = _rms_norm(x, g_ref[...]).astype(BF16)

    ang = invf_ref[...] * pos_ref[...].astype(F32)
    sel = (lax.broadcasted_iota(jnp.int32, (3 * ROPE_HALF, LANES), 0) % ROPE_HALF
           == lax.broadcasted_iota(jnp.int32, (3 * ROPE_HALF, LANES), 1) % ROPE_HALF).astype(F32)

    def spread(c):
        hi = c.astype(BF16).astype(F32)
        mid = (c - hi).astype(BF16).astype(F32)
        lo = (c - hi - mid).astype(BF16).astype(F32)
        return _dot_tn(jnp.concatenate([hi, mid, lo], axis=0), sel)

    lane = lax.broadcasted_iota(jnp.int32, (1, LANES), 1)
    head_lane = lane % SWA_HEAD_DIM
    sin_t = spread(jnp.sin(ang))
    cos_t = jnp.where(head_lane < ROPE_DIM, spread(jnp.cos(ang)), 1.0)
    sin_s = jnp.where(head_lane < ROPE_HALF, -sin_t, jnp.where(head_lane < ROPE_DIM, sin_t, 0.0))

    src = lax.broadcasted_iota(jnp.int32, (LANES, LANES), 0)
    dst = lax.broadcasted_iota(jnp.int32, (LANES, LANES), 1)
    dst_head_lane = dst % SWA_HEAD_DIM
    swap = (((dst_head_lane < ROPE_HALF) & (src == dst + ROPE_HALF))
            | ((dst_head_lane >= ROPE_HALF) & (dst_head_lane < ROPE_DIM) & (src == dst - ROPE_HALF)))
    swap2 = jnp.concatenate([swap.astype(BF16), swap.astype(BF16)], axis=0)

    kv = _dot(xn, wqkv_ref[:, d_q:d_q + 2 * d_kv]) + bqkv_ref[:, d_q:d_q + 2 * d_kv]
    tiles = [kv[:, :d_kv]]
    q_cols = 2 * LANES
    for c in range(d_q // q_cols):
        q2 = _dot(xn, wqkv_ref[:, c * q_cols:(c + 1) * q_cols]) + bqkv_ref[:, c * q_cols:(c + 1) * q_cols]
        tiles += [q2[:, t * LANES:(t + 1) * LANES] for t in range(q_cols // LANES)]

    def partner_dot(t):
        hi = t.astype(BF16)
        lo = (t - hi.astype(F32)).astype(BF16)
        return _dot(jnp.concatenate([hi, lo], axis=1), swap2)

    partners = [partner_dot(t) for t in tiles]
    roped = [t * cos_t + p * sin_s for t, p in zip(tiles, partners)]

    low_half = lane < SWA_HEAD_DIM
    cur = slice(SWA_BLOCK, SWA_BLOCK + rows)
    for t, lo_ref, hi_ref in ((roped[0], klo_ref, khi_ref), (kv[:, d_kv:], vlo_ref, vhi_ref)):
        swapped = pltpu.roll(t, SWA_HEAD_DIM, 1)
        lo_ref[0, cur, 0:LANES] = jnp.where(low_half, t, 0.0).astype(BF16)
        hi_ref[0, cur, 0:LANES] = jnp.where(low_half, 0.0, swapped).astype(BF16)
        lo_ref[1, cur, 0:LANES] = jnp.where(low_half, swapped, 0.0).astype(BF16)
        hi_ref[1, cur, 0:LANES] = jnp.where(low_half, 0.0, t).astype(BF16)
    ones_lo = jnp.broadcast_to(jnp.where(low_half, 1.0, 0.0), (rows, LANES)).astype(BF16)
    ones_hi = jnp.broadcast_to(jnp.where(low_half, 0.0, 1.0), (rows, LANES)).astype(BF16)
    for h in range(SWA_KV_HEADS):
        vlo_ref[h, cur, LANES:2 * LANES] = ones_lo
        vhi_ref[h, cur, LANES:2 * LANES] = ones_hi

    for tile, q_rot in enumerate(roped[1:]):
        kvh, slot = divmod(tile, tiles_per_kv)
        q_t = (q_rot * (SWA_HEAD_DIM ** -0.5 * LOG2_E)).astype(BF16)
        for j in range(n_blocks):
            q_ref[kvh, j, slot * SWA_BLOCK:(slot + 1) * SWA_BLOCK] = q_t[j * SWA_BLOCK:(j + 1) * SWA_BLOCK]

    qi = lax.broadcasted_iota(jnp.int32, (SWA_BLOCK, 2 * SWA_BLOCK), 0)
    kj = lax.broadcasted_iota(jnp.int32, (SWA_BLOCK, 2 * SWA_BLOCK), 1)
    delta = qi + SWA_BLOCK - kj
    in_window = (delta >= 0) & (delta < SWA_BLOCK)

    items = [(j, kvh) for j in range(n_blocks) for kvh in range(SWA_KV_HEADS)]

    def score_dots(j, kvh):
        band = slice(j * SWA_BLOCK, (j + 2) * SWA_BLOCK)
        qs = q_ref[kvh, j]
        return [_dot_nt(qs, k_ref[kvh, band]) for k_ref in (klo_ref, khi_ref)]

    s_next = score_dots(*items[0])
    for n, (j, kvh) in enumerate(items):
        rs = slice(j * SWA_BLOCK, (j + 1) * SWA_BLOCK)
        band = slice(j * SWA_BLOCK, (j + 2) * SWA_BLOCK)
        if j == 0:
            valid = in_window & (kj >= jnp.where(step > 0, 0, SWA_BLOCK))
        else:
            valid = in_window
        s_cur = s_next
        if n + 1 < len(items):
            s_next = score_dots(*items[n + 1])
        buf = n % 2
        sink_terms = []
        for e in range(2):
            terms_e = []
            for slot in range(tiles_per_kv):
                ts = slice(slot * SWA_BLOCK, (slot + 1) * SWA_BLOCK)
                sink = sinks_ref[(kvh * tiles_per_kv + slot) * 2 + e] * LOG2_E
                scores = jnp.where(valid, s_cur[e][ts], MASKED_SCORE)
                m = jnp.maximum(jnp.max(scores, axis=-1, keepdims=True), sink)
                p_ref[buf, e, ts] = jnp.exp2(scores - m).astype(BF16)
                terms_e.append(jnp.exp2(sink - m))
            sink_terms.append(terms_e)
        out = _dot(p_ref[buf, 0], vlo_ref[kvh, band]) + _dot(p_ref[buf, 1], vhi_ref[kvh, band])
        for slot in range(tiles_per_kv):
            ts = slice(slot * SWA_BLOCK, (slot + 1) * SWA_BLOCK)
            tile = kvh * tiles_per_kv + slot
            denom = out[ts, LANES:2 * LANES] + jnp.where(low_half, sink_terms[0][slot], sink_terms[1][slot])
            attn_ref[rs, tile * LANES:(tile + 1) * LANES] = (out[ts, 0:LANES] / denom).astype(BF16)
    o_ref[...] = x + _dot(attn_ref[...], wo_ref[...]) + bo_ref[...]


def _swa(x3, positions, norm, w_qkv, b_qkv, sinks, w_out, b_out, *, casts=()):
    bsz, s, d = x3.shape
    rows = min(SWA_ROWS, s)
    d_q = SWA_HEADS * SWA_HEAD_DIM
    d_kv = SWA_KV_HEADS * SWA_HEAD_DIM
    n_qkv = d_q + 2 * d_kv
    invf = (ROPE_THETA ** (-jnp.arange(0, ROPE_DIM, 2, dtype=F32) / ROPE_DIM)).reshape(ROPE_HALF, 1)
    xspec = pl.BlockSpec((None, rows, d), lambda b, i: (b, i, 0))

    def kv_buffer(width):
        return pltpu.VMEM((SWA_KV_HEADS, SWA_BLOCK + rows, width), BF16)

    return _sub_block_call(
        functools.partial(_swa_body, rows=rows),
        grid=(bsz, s // rows),
        semantics=("arbitrary", "arbitrary"),
        in_specs=[
            xspec,
            pl.BlockSpec((None, 1, rows), lambda b, i: (b, 0, i)),
            _resident((1, d)),
            _resident((d, n_qkv)),
            _resident((1, n_qkv)),
            _resident((ROPE_HALF, 1)),
            pl.BlockSpec(memory_space=pltpu.SMEM),
            _resident((d_q, d)),
            _resident((1, d)),
        ],
        out_spec=xspec,
        out_shape=jax.ShapeDtypeStruct((bsz, s, d), F32),
        scratch_shapes=[
            kv_buffer(LANES), kv_buffer(LANES), kv_buffer(2 * LANES), kv_buffer(2 * LANES),
            pltpu.VMEM((SWA_KV_HEADS, rows // SWA_BLOCK, d_q // SWA_KV_HEADS, LANES), BF16),
            pltpu.VMEM((2, 2, d_q // SWA_KV_HEADS, 2 * SWA_BLOCK), BF16),
            pltpu.VMEM((rows, d_q), BF16),
        ],
        name="swa",
        args=(x3, positions.reshape(bsz, 1, s), norm.reshape(1, d), w_qkv, b_qkv.reshape(1, n_qkv),
              invf, sinks, w_out, b_out.reshape(1, d)),
        casts=casts,
    )


def kernel(x, positions, l0_ffn1_norm, l0_ffn1_w_in, l0_ffn1_w_out, l0_mix_norm, l0_gla_w_in, l0_gla_w_gk_up, l0_gla_b_gk, l0_gla_o_norm, l0_gla_w_out, l0_ffn2_norm, l0_ffn2_w_in, l0_ffn2_w_out, l1_ffn1_norm, l1_ffn1_w_in, l1_ffn1_w_out, l1_mix_norm, l1_sgu_w_in, l1_sgu_ln_gain, l1_sgu_ln_bias, l1_sgu_w_s, l1_sgu_b_s, l1_sgu_w_out, l1_ffn2_norm, l1_ffn2_w_in, l1_ffn2_w_out, l2_ffn1_norm, l2_ffn1_w_in, l2_ffn1_w_out, l2_mix_norm, l2_swa_w_qkv, l2_swa_b_qkv, l2_swa_sinks, l2_swa_w_out, l2_swa_b_out, l2_ffn2_norm, l2_ffn2_w_in, l2_ffn2_w_out, l3_ffn1_norm, l3_ffn1_w_in, l3_ffn1_w_out, l3_mix_norm, l3_gla_w_in, l3_gla_w_gk_up, l3_gla_b_gk, l3_gla_o_norm, l3_gla_w_out, l3_ffn2_norm, l3_ffn2_w_in, l3_ffn2_w_out, final_norm):
    bsz, s, d = x.shape

    def ffn(t, norm, w_bf16, casts, final=False):
        y, nxt = _ffn(t.reshape(bsz * s, d), norm, *w_bf16, final_norm, final=final, casts=casts)
        return y.reshape(bsz, s, d), nxt

    def ffn_casts(w_in, w_out):
        return [_whole(w_in), _whole(w_out)]

    w = [l0_ffn1_w_in.astype(BF16), l0_ffn1_w_out.astype(BF16)]
    x, w = ffn(x, l0_ffn1_norm, w, _gla_casts(l0_gla_w_in, l0_gla_w_out))
    x, w = _gla(x, l0_mix_norm, w[0], w[1], l0_gla_w_gk_up, l0_gla_b_gk, l0_gla_o_norm, w[2],
                casts=ffn_casts(l0_ffn2_w_in, l0_ffn2_w_out))
    x, w = ffn(x, l0_ffn2_norm, w, ffn_casts(l1_ffn1_w_in, l1_ffn1_w_out))
    x, w = ffn(x, l1_ffn1_norm, w, ffn_casts(l1_sgu_w_in, l1_sgu_w_out))
    x, w = _sgu(x.reshape(bsz * s, d), l1_mix_norm, w[0], l1_sgu_ln_gain, l1_sgu_ln_bias,
                l1_sgu_w_s, l1_sgu_b_s, w[1], casts=ffn_casts(l1_ffn2_w_in, l1_ffn2_w_out))
    x = x.reshape(bsz, s, d)
    x, w = ffn(x, l1_ffn2_norm, w, ffn_casts(l2_ffn1_w_in, l2_ffn1_w_out))
    x, w = ffn(x, l2_ffn1_norm, w, ffn_casts(l2_swa_w_qkv, l2_swa_w_out))
    x, w = _swa(x, positions, l2_mix_norm, w[0], l2_swa_b_qkv, l2_swa_sinks, w[1], l2_swa_b_out,
                casts=ffn_casts(l2_ffn2_w_in, l2_ffn2_w_out))
    x, w = ffn(x, l2_ffn2_norm, w, ffn_casts(l3_ffn1_w_in, l3_ffn1_w_out))
    x, w = ffn(x, l3_ffn1_norm, w, _gla_casts(l3_gla_w_in, l3_gla_w_out))
    x, w = _gla(x, l3_mix_norm, w[0], w[1], l3_gla_w_gk_up, l3_gla_b_gk, l3_gla_o_norm, w[2],
                casts=ffn_casts(l3_ffn2_w_in, l3_ffn2_w_out))
    x, _ = ffn(x, l3_ffn2_norm, w, [], final=True)
    return x
```

```python
import functools

import jax
import jax.numpy as jnp
from jax import lax
from jax.experimental import pallas as pl
from jax.experimental.pallas import tpu as pltpu

F32 = jnp.float32
BF16 = jnp.bfloat16

NORM_EPS = 1e-5
LANES = 128
VMEM_LIMIT_BYTES = 56 * 1024 * 1024

GLA_HEADS = 4
GLA_HEAD_K = 128
GLA_HEAD_V = 256
GLA_DK = GLA_HEADS * GLA_HEAD_K
GLA_DV = GLA_HEADS * GLA_HEAD_V
GLA_GATE_RANK = 16
GLA_GATE_NORMALIZER = 16.0
GLA_CHUNK = 64
SGU_GROUPS = 8
SGU_CHUNK = 128
SWA_HEADS = 16
SWA_KV_HEADS = 2
SWA_HEAD_DIM = 64
SWA_GROUP = SWA_HEADS // SWA_KV_HEADS
SWA_BLOCK = 128
ROPE_DIM = SWA_HEAD_DIM // 4
ROPE_HALF = ROPE_DIM // 2
ROPE_THETA = 500000.0
MASKED_SCORE = -1e30
LOG2_E = 1.4426950408889634


def _dot(a, b):
    return jnp.dot(a, b, preferred_element_type=F32)


def _dot_nt(a, b):
    return lax.dot_general(a, b, (((1,), (1,)), ((), ())), preferred_element_type=F32)


def _dot_tn(a, b):
    return lax.dot_general(a, b, (((0,), (0,)), ((), ())), preferred_element_type=F32)


def _rms_norm(x, gain):
    return x * lax.rsqrt(jnp.mean(x * x, axis=-1, keepdims=True) + NORM_EPS) * gain


def _silu(x):
    return x * (1.0 / (1.0 + jnp.exp(-x)))


def _resident(shape):
    zeros = (0,) * len(shape)
    return pl.BlockSpec(shape, lambda *_: zeros, pipeline_mode=pl.Buffered(1))


def _compiler_params(semantics):
    return pltpu.CompilerParams(dimension_semantics=semantics,
                                vmem_limit_bytes=VMEM_LIMIT_BYTES)


BF16_SUBLANES = 16


def _whole(w):
    return (w, [(0, w.shape[1], w.shape[1])])


def _cast_rows(n_rows, n_steps):
    for rb in range(BF16_SUBLANES, n_rows + 1, BF16_SUBLANES):
        if n_rows % rb == 0 and n_rows // rb <= n_steps:
            return rb
    raise ValueError("no aligned row block for %d rows in %d steps" % (n_rows, n_steps))


def _cast_blocks(in_refs, out_refs, col_plan):
    outs = iter(out_refs)
    for ref, cols in zip(in_refs, col_plan):
        for lo, hi, width in cols:
            blk = ref[:, lo:hi]
            if width > hi - lo:
                blk = jnp.concatenate([blk, jnp.zeros((blk.shape[0], width - (hi - lo)), F32)], axis=1)
            next(outs)[...] = blk.astype(BF16)


def _sub_block_call(body, *, grid, semantics, in_specs, out_spec, out_shape, scratch_shapes,
                    name, args, casts):
    n_steps = 1
    for g in grid:
        n_steps *= g

    def step_of(*idx):
        step = idx[0]
        for i, g in zip(idx[1:], grid[1:]):
            step = step * g + i
        return step

    cast_in_specs, cast_out_specs, cast_out_shapes, col_plan = [], [], [], []
    for arr, cols in casts:
        n_rows, n_cols = arr.shape
        rb = _cast_rows(n_rows, n_steps)
        n_blocks = n_rows // rb
        stride = n_steps // n_blocks

        def index_map(*idx, stride=stride, n_blocks=n_blocks):
            return (jnp.minimum(step_of(*idx) // stride, n_blocks - 1), 0)

        cast_in_specs.append(pl.BlockSpec((rb, n_cols), index_map))
        col_plan.append(cols)
        for _, _, width in cols:
            cast_out_specs.append(pl.BlockSpec((rb, width), index_map))
            cast_out_shapes.append(jax.ShapeDtypeStruct((n_rows, width), BF16))

    n_in, n_cast_in, n_cast_out = len(in_specs), len(casts), len(cast_out_specs)

    def body_with_casts(*refs):
        main_in = refs[:n_in]
        cast_in = refs[n_in:n_in + n_cast_in]
        out_ref = refs[n_in + n_cast_in]
        cast_out = refs[n_in + n_cast_in + 1:n_in + n_cast_in + 1 + n_cast_out]
        scratch = refs[n_in + n_cast_in + 1 + n_cast_out:]
        _cast_blocks(cast_in, cast_out, col_plan)
        body(*main_in, out_ref, *scratch)

    outs = pl.pallas_call(
        body_with_casts,
        grid=grid,
        in_specs=list(in_specs) + cast_in_specs,
        out_specs=[out_spec] + cast_out_specs,
        out_shape=[out_shape] + cast_out_shapes,
        scratch_shapes=scratch_shapes,
        compiler_params=_compiler_params(semantics),
        name=name,
    )(*args, *[arr for arr, _ in casts])
    return outs[0], list(outs[1:])


FFN_ROWS = 1024
FFN_COLS = 256


def _ffn_body(x_ref, g_ref, wi_ref, wo_ref, fg_ref, o_ref, act_ref, *, d_ff, final):
    x = x_ref[...]
    xn = _rms_norm(x, g_ref[...]).astype(BF16)
    for c in range(d_ff // FFN_COLS):
        lo = c * FFN_COLS
        gate = _dot(xn, wi_ref[:, lo:lo + FFN_COLS])
        up = _dot(xn, wi_ref[:, d_ff + lo:d_ff + lo + FFN_COLS])
        act_ref[:, lo:lo + FFN_COLS] = (_silu(gate) * up).astype(BF16)
    y = x + 0.5 * _dot(act_ref[...], wo_ref[...])
    if final:
        y = _rms_norm(y, fg_ref[...])
    o_ref[...] = y


def _ffn(x2, norm, w_in, w_out, final_gain, *, final, casts=()):
    t, d = x2.shape
    d_ff = w_out.shape[0]
    rows = min(FFN_ROWS, t)
    return _sub_block_call(
        functools.partial(_ffn_body, d_ff=d_ff, final=final),
        grid=(t // rows,),
        semantics=("arbitrary",),
        in_specs=[
            pl.BlockSpec((rows, d), lambda i: (i, 0)),
            _resident((1, d)),
            _resident((d, 2 * d_ff)),
            _resident((d_ff, d)),
            _resident((1, d)),
        ],
        out_spec=pl.BlockSpec((rows, d), lambda i: (i, 0)),
        out_shape=jax.ShapeDtypeStruct((t, d), F32),
        scratch_shapes=[pltpu.VMEM((rows, d_ff), BF16)],
        name="ffn_final" if final else "ffn",
        args=(x2, norm.reshape(1, d), w_in, w_out, final_gain.reshape(1, d)),
        casts=casts,
    )


GLA_ROWS = 512


def _gla_body(x_ref, g_ref, wq_ref, wgk_ref, wup_ref, bgk_ref, on_ref, wo_ref, o_ref,
              state_ref, qa_ref, ka_ref, qs_ref, ks_ref, v_ref, attn_ref, oacc_ref, gate_ref, *, rows):

    @pl.when(pl.program_id(1) == 0)
    def _():
        state_ref[...] = jnp.zeros_like(state_ref)

    x = x_ref[...]
    xn = _rms_norm(x, g_ref[...]).astype(BF16)
    gk_low = _dot(xn, wgk_ref[...])
    v_ref[...] = _dot(xn, wq_ref[:, 2 * GLA_DK:2 * GLA_DK + GLA_DV]).astype(BF16)
    z = _dot(gk_low.astype(BF16), wup_ref[...]) + bgk_ref[...]
    gate_ref[...] = _silu(_dot(xn, wq_ref[:, 2 * GLA_DK + GLA_DV:2 * GLA_DK + 2 * GLA_DV]))
    log_a = (jnp.minimum(z, 0.0) - jnp.log1p(jnp.exp(-jnp.abs(z)))) * (1.0 / GLA_GATE_NORMALIZER)
    q = _dot(xn, wq_ref[:, 0:GLA_DK]) * (GLA_HEAD_K ** -0.5)
    k = _dot(xn, wq_ref[:, GLA_DK:2 * GLA_DK])

    ri = lax.broadcasted_iota(jnp.int32, (GLA_CHUNK, GLA_CHUNK), 0)
    ci = lax.broadcasted_iota(jnp.int32, (GLA_CHUNK, GLA_CHUNK), 1)
    causal = ri >= ci
    tril = causal.astype(BF16)
    tril3 = jnp.concatenate([tril, tril, tril], axis=1)
    la_hi = log_a.astype(BF16)
    rest = log_a - la_hi.astype(F32)
    la_mid = rest.astype(BF16)
    la_lo = (rest - la_mid.astype(F32)).astype(BF16)

    pair = 2 * GLA_CHUNK
    pair_decays = []
    for p in range(rows // pair):
        r1 = slice(p * pair, p * pair + GLA_CHUNK)
        r2 = slice(p * pair + GLA_CHUNK, (p + 1) * pair)
        b1 = _dot(tril3, jnp.concatenate([la_hi[r1], la_mid[r1], la_lo[r1]], axis=0))
        b2 = _dot(tril3, jnp.concatenate([la_hi[r2], la_mid[r2], la_lo[r2]], axis=0))
        bl1 = b1[GLA_CHUNK - 1:GLA_CHUNK]
        bl2 = b2[GLA_CHUNK - 1:GLA_CHUNK]
        d1 = jnp.exp(bl1)
        d2 = jnp.exp(bl2)
        qd1 = q[r1] * jnp.exp(b1)
        qd2 = q[r2] * jnp.exp(b2)
        ks1 = k[r1] * jnp.exp(bl1 - b1)
        ks2 = k[r2] * jnp.exp(bl2 - b2)
        qa_ref[r1] = (q[r1] * jnp.exp(b1 - bl1)).astype(BF16)
        qa_ref[r2] = qd2.astype(BF16)
        ka_ref[r1] = ks1.astype(BF16)
        ka_ref[r2] = (k[r2] * jnp.exp(-b2)).astype(BF16)
        qs_ref[r1] = qd1.astype(BF16)
        qs_ref[r2] = (qd2 * d1).astype(BF16)
        ks_ref[r1] = (ks1 * d2).astype(BF16)
        ks_ref[r2] = ks2.astype(BF16)
        pair_decays.append(d1 * d2)

    pi = lax.broadcasted_iota(jnp.int32, (pair, pair), 0)
    pj = lax.broadcasted_iota(jnp.int32, (pair, pair), 1)
    pair_causal = pi >= pj
    for p in range(rows // pair):
        rs = slice(p * pair, (p + 1) * pair)
        for h in range(GLA_HEADS):
            ks_ = slice(h * GLA_HEAD_K, (h + 1) * GLA_HEAD_K)
            attn = jnp.where(pair_causal, _dot_nt(qa_ref[rs, ks_], ka_ref[rs, ks_]), 0.0)
            attn_ref[h, rs] = attn.astype(BF16)

    states = [state_ref[h] for h in range(GLA_HEADS)]
    for p in range(rows // pair):
        rs = slice(p * pair, (p + 1) * pair)
        for h in range(GLA_HEADS):
            ks_ = slice(h * GLA_HEAD_K, (h + 1) * GLA_HEAD_K)
            vs_ = slice(h * GLA_HEAD_V, (h + 1) * GLA_HEAD_V)
            v_h = v_ref[rs, vs_]
            oacc_ref[rs, vs_] = (_dot(attn_ref[h, rs], v_h)
                                 + _dot_nt(qs_ref[rs, ks_], states[h].astype(BF16)))
            states[h] = states[h] * pair_decays[p][:, ks_] + _dot_tn(v_h, ks_ref[rs, ks_])
    for h in range(GLA_HEADS):
        state_ref[h] = states[h]

    gain = on_ref[...]
    y = x
    for h in range(GLA_HEADS):
        vs_ = slice(h * GLA_HEAD_V, (h + 1) * GLA_HEAD_V)
        o_h = _rms_norm(oacc_ref[:, vs_], gain) * gate_ref[:, vs_]
        y = y + _dot(o_h.astype(BF16), wo_ref[vs_, :])
    o_ref[...] = y


GLA_N_QKVR = 2 * GLA_DK + 2 * GLA_DV


def _gla_casts(w_in, w_out):
    return [(w_in, [(0, GLA_N_QKVR, GLA_N_QKVR), (GLA_N_QKVR, w_in.shape[1], LANES)]), _whole(w_out)]


def _gla(x3, norm, w_qkvr, w_gk, w_gk_up, b_gk, o_norm, w_out, *, casts=()):
    bsz, s, d = x3.shape
    rows = min(GLA_ROWS, s)
    w_up = jnp.pad(w_gk_up, ((0, LANES - GLA_GATE_RANK), (0, 0))).astype(BF16)
    xspec = pl.BlockSpec((None, rows, d), lambda b, i: (b, i, 0))
    return _sub_block_call(
        functools.partial(_gla_body, rows=rows),
        grid=(bsz, s // rows),
        semantics=("arbitrary", "arbitrary"),
        in_specs=[
            xspec,
            _resident((1, d)),
            _resident((d, GLA_N_QKVR)),
            _resident((d, LANES)),
            _resident((LANES, GLA_DK)),
            _resident((1, GLA_DK)),
            _resident((1, GLA_HEAD_V)),
            _resident((GLA_DV, d)),
        ],
        out_spec=xspec,
        out_shape=jax.ShapeDtypeStruct((bsz, s, d), F32),
        scratch_shapes=[
            pltpu.VMEM((GLA_HEADS, GLA_HEAD_V, GLA_HEAD_K), F32),
            pltpu.VMEM((rows, GLA_DK), BF16),
            pltpu.VMEM((rows, GLA_DK), BF16),
            pltpu.VMEM((rows, GLA_DK), BF16),
            pltpu.VMEM((rows, GLA_DK), BF16),
            pltpu.VMEM((rows, GLA_DV), BF16),
            pltpu.VMEM((GLA_HEADS, rows, 2 * GLA_CHUNK), BF16),
            pltpu.VMEM((rows, GLA_DV), F32),
            pltpu.VMEM((rows, GLA_DV), F32),
        ],
        name="gla",
        args=(x3, norm.reshape(1, d), w_qkvr, w_gk, w_up, b_gk.reshape(1, GLA_DK),
              o_norm.reshape(1, GLA_HEAD_V), w_out),
        casts=casts,
    )


SGU_ROWS = 512
SGU_COLS = 768


def _gelu(x):
    return 0.5 * x * (1.0 + lax.erf(x * (2.0 ** -0.5)))


def _sgu_body(x_ref, g_ref, wi_ref, lng_ref, lnb_ref, ws_ref, bs_ref, wo_ref, o_ref,
              v_ref, uv_ref, *, rows, half):
    x = x_ref[...]
    xn = _rms_norm(x, g_ref[...]).astype(BF16)
    n_col = half // SGU_COLS

    total = jnp.zeros((rows, 1), F32)
    for c in range(n_col):
        lo = c * SGU_COLS
        v_c = _gelu(_dot(xn, wi_ref[:, half + lo:half + lo + SGU_COLS]))
        v_ref[:, lo:lo + SGU_COLS] = v_c
        total = total + jnp.sum(v_c, axis=-1, keepdims=True)
    mu = total * (1.0 / half)
    sq = jnp.zeros((rows, 1), F32)
    for c in range(n_col):
        lo = c * SGU_COLS
        d_c = v_ref[:, lo:lo + SGU_COLS] - mu
        sq = sq + jnp.sum(d_c * d_c, axis=-1, keepdims=True)
    inv = lax.rsqrt(sq * (1.0 / half) + NORM_EPS)

    ri = lax.broadcasted_iota(jnp.int32, (SGU_CHUNK, SGU_CHUNK), 0)
    ci = lax.broadcasted_iota(jnp.int32, (SGU_CHUNK, SGU_CHUNK), 1)
    causal = ri >= ci
    group_dim = half // SGU_GROUPS
    n_chunks = rows // SGU_CHUNK
    for c in range(n_col):
        u_c = _gelu(_dot(xn, wi_ref[:, c * SGU_COLS:(c + 1) * SGU_COLS]))
        for g in range(c * SGU_COLS // group_dim, (c + 1) * SGU_COLS // group_dim):
            lo = g * group_dim
            cols = slice(lo, lo + group_dim)
            vn = ((v_ref[:, cols] - mu) * inv * lng_ref[:, cols] + lnb_ref[:, cols]).astype(BF16)
            w_g = jnp.where(causal, ws_ref[g], 0.0).astype(BF16)
            bias = bs_ref[:, g:g + 1]
            vn_wide = jnp.concatenate(
                [vn[j * SGU_CHUNK:(j + 1) * SGU_CHUNK] for j in range(n_chunks)], axis=1)
            mixed = _dot(w_g, vn_wide) + bias
            for j in range(n_chunks):
                rs = slice(j * SGU_CHUNK, (j + 1) * SGU_CHUNK)
                u_g = u_c[rs, lo - c * SGU_COLS:lo - c * SGU_COLS + group_dim]
                uv_ref[rs, cols] = (u_g * mixed[:, j * group_dim:(j + 1) * group_dim]).astype(BF16)
    o_ref[...] = x + _dot(uv_ref[...], wo_ref[...])


def _sgu(x2, norm, w_in, ln_gain, ln_bias, w_s, b_s, w_out, *, casts=()):
    t, d = x2.shape
    half = w_out.shape[0]
    rows = min(SGU_ROWS, t)
    return _sub_block_call(
        functools.partial(_sgu_body, rows=rows, half=half),
        grid=(t // rows,),
        semantics=("arbitrary",),
        in_specs=[
            pl.BlockSpec((rows, d), lambda i: (i, 0)),
            _resident((1, d)),
            _resident((d, 2 * half)),
            _resident((1, half)),
            _resident((1, half)),
            _resident((SGU_GROUPS, SGU_CHUNK, SGU_CHUNK)),
            _resident((SGU_CHUNK, SGU_GROUPS)),
            _resident((half, d)),
        ],
        out_spec=pl.BlockSpec((rows, d), lambda i: (i, 0)),
        out_shape=jax.ShapeDtypeStruct((t, d), F32),
        scratch_shapes=[
            pltpu.VMEM((rows, half), F32),
            pltpu.VMEM((rows, half), BF16),
        ],
        name="sgu",
        args=(x2, norm.reshape(1, d), w_in, ln_gain.reshape(1, half), ln_bias.reshape(1, half),
              w_s, b_s.T, w_out),
        casts=casts,
    )


SWA_ROWS = 512


def _swa_body(x_ref, pos_ref, g_ref, wqkv_ref, bqkv_ref, invf_ref, sinks_ref, wo_ref, bo_ref,
              o_ref, klo_ref, khi_ref, vlo_ref, vhi_ref, q_ref, p_ref, attn_ref, *, rows):
    d_q = SWA_HEADS * SWA_HEAD_DIM
    d_kv = SWA_KV_HEADS * SWA_HEAD_DIM
    n_blocks = rows // SWA_BLOCK
    tiles_per_kv = SWA_GROUP * SWA_HEAD_DIM // LANES
    step = pl.program_id(1)
    kv_refs = (klo_ref, khi_ref, vlo_ref, vhi_ref)

    @pl.when(step == 0)
    def _():
        for ref in kv_refs:
            ref[:, 0:SWA_BLOCK] = jnp.zeros((SWA_KV_HEADS, SWA_BLOCK, ref.shape[-1]), BF16)

    @pl.when(step > 0)
    def _():
        for ref in kv_refs:
            ref[:, 0:SWA_BLOCK] = ref[:, rows:rows + SWA_BLOCK]

    x = x_ref[...]
    xn = _rms_norm(x, g_ref[...]).astype(BF16)

    kv = _dot(xn, wqkv_ref[:, d_q:d_q + 2 * d_kv]) + bqkv_ref[:, d_q:d_q + 2 * d_kv]
    tiles = [kv[:, :d_kv]]
    q_cols = 2 * LANES
    for c in range(d_q // q_cols):
        q2 = _dot(xn, wqkv_ref[:, c * q_cols:(c + 1) * q_cols]) + bqkv_ref[:, c * q_cols:(c + 1) * q_cols]
        tiles += [q2[:, t * LANES:(t + 1) * LANES] for t in range(q_cols // LANES)]

    ang = invf_ref[...] * pos_ref[...].astype(F32)
    sel = (lax.broadcasted_iota(jnp.int32, (3 * ROPE_HALF, LANES), 0) % ROPE_HALF
           == lax.broadcasted_iota(jnp.int32, (3 * ROPE_HALF, LANES), 1) % ROPE_HALF).astype(F32)

    def spread(c):
        hi = c.astype(BF16).astype(F32)
        mid = (c - hi).astype(BF16).astype(F32)
        lo = (c - hi - mid).astype(BF16).astype(F32)
        return _dot_tn(jnp.concatenate([hi, mid, lo], axis=0), sel)

    lane = lax.broadcasted_iota(jnp.int32, (1, LANES), 1)
    head_lane = lane % SWA_HEAD_DIM
    sin_t = spread(jnp.sin(ang))
    cos_t = jnp.where(head_lane < ROPE_DIM, spread(jnp.cos(ang)), 1.0)
    sin_s = jnp.where(head_lane < ROPE_HALF, -sin_t, jnp.where(head_lane < ROPE_DIM, sin_t, 0.0))

    src = lax.broadcasted_iota(jnp.int32, (LANES, LANES), 0)
    dst = lax.broadcasted_iota(jnp.int32, (LANES, LANES), 1)
    dst_head_lane = dst % SWA_HEAD_DIM
    swap = (((dst_head_lane < ROPE_HALF) & (src == dst + ROPE_HALF))
            | ((dst_head_lane >= ROPE_HALF) & (dst_head_lane < ROPE_DIM) & (src == dst - ROPE_HALF)))
    swap2 = jnp.concatenate([swap.astype(BF16), swap.astype(BF16)], axis=0)

    def partner_dot(t):
        hi = t.astype(BF16)
        lo = (t - hi.astype(F32)).astype(BF16)
        return _dot(jnp.concatenate([hi, lo], axis=1), swap2)

    partners = [partner_dot(t) for t in tiles]
    roped = [t * cos_t + p * sin_s for t, p in zip(tiles, partners)]

    low_half = lane < SWA_HEAD_DIM
    cur = slice(SWA_BLOCK, SWA_BLOCK + rows)
    for t, lo_ref, hi_ref in ((roped[0], klo_ref, khi_ref), (kv[:, d_kv:], vlo_ref, vhi_ref)):
        swapped = pltpu.roll(t, SWA_HEAD_DIM, 1)
        lo_ref[0, cur, 0:LANES] = jnp.where(low_half, t, 0.0).astype(BF16)
        hi_ref[0, cur, 0:LANES] = jnp.where(low_half, 0.0, swapped).astype(BF16)
        lo_ref[1, cur, 0:LANES] = jnp.where(low_half, swapped, 0.0).astype(BF16)
        hi_ref[1, cur, 0:LANES] = jnp.where(low_half, 0.0, t).astype(BF16)
    ones_lo = jnp.broadcast_to(jnp.where(low_half, 1.0, 0.0), (rows, LANES)).astype(BF16)
    ones_hi = jnp.broadcast_to(jnp.where(low_half, 0.0, 1.0), (rows, LANES)).astype(BF16)
    for h in range(SWA_KV_HEADS):
        vlo_ref[h, cur, LANES:2 * LANES] = ones_lo
        vhi_ref[h, cur, LANES:2 * LANES] = ones_hi

    for tile, q_rot in enumerate(roped[1:]):
        kvh, slot = divmod(tile, tiles_per_kv)
        q_t = (q_rot * (SWA_HEAD_DIM ** -0.5 * LOG2_E)).astype(BF16)
        for j in range(n_blocks):
            q_ref[kvh, j, slot * SWA_BLOCK:(slot + 1) * SWA_BLOCK] = q_t[j * SWA_BLOCK:(j + 1) * SWA_BLOCK]

    qi = lax.broadcasted_iota(jnp.int32, (SWA_BLOCK, 2 * SWA_BLOCK), 0)
    kj = lax.broadcasted_iota(jnp.int32, (SWA_BLOCK, 2 * SWA_BLOCK), 1)
    delta = qi + SWA_BLOCK - kj
    in_window = (delta >= 0) & (delta < SWA_BLOCK)

    items = [(j, kvh) for j in range(n_blocks) for kvh in range(SWA_KV_HEADS)]

    def score_dots(j, kvh):
        band = slice(j * SWA_BLOCK, (j + 2) * SWA_BLOCK)
        qs = q_ref[kvh, j]
        return [_dot_nt(qs, k_ref[kvh, band]) for k_ref in (klo_ref, khi_ref)]

    s_next = score_dots(*items[0])
    for n, (j, kvh) in enumerate(items):
        rs = slice(j * SWA_BLOCK, (j + 1) * SWA_BLOCK)
        band = slice(j * SWA_BLOCK, (j + 2) * SWA_BLOCK)
        if j == 0:
            valid = in_window & (kj >= jnp.where(step > 0, 0, SWA_BLOCK))
        else:
            valid = in_window
        s_cur = s_next
        if n + 1 < len(items):
            s_next = score_dots(*items[n + 1])
        buf = n % 2
        sink_terms = []
        for e in range(2):
            terms_e = []
            for slot in range(tiles_per_kv):
                ts = slice(slot * SWA_BLOCK, (slot + 1) * SWA_BLOCK)
                sink = sinks_ref[(kvh * tiles_per_kv + slot) * 2 + e] * LOG2_E
                scores = jnp.where(valid, s_cur[e][ts], MASKED_SCORE)
                m = jnp.maximum(jnp.max(scores, axis=-1, keepdims=True), sink)
                p_ref[buf, e, ts] = jnp.exp2(scores - m).astype(BF16)
                terms_e.append(jnp.exp2(sink - m))
            sink_terms.append(terms_e)
        out = _dot(p_ref[buf, 0], vlo_ref[kvh, band]) + _dot(p_ref[buf, 1], vhi_ref[kvh, band])
        for slot in range(tiles_per_kv):
            ts = slice(slot * SWA_BLOCK, (slot + 1) * SWA_BLOCK)
            tile = kvh * tiles_per_kv + slot
            denom = out[ts, LANES:2 * LANES] + jnp.where(low_half, sink_terms[0][slot], sink_terms[1][slot])
            attn_ref[rs, tile * LANES:(tile + 1) * LANES] = (out[ts, 0:LANES] / denom).astype(BF16)
    o_ref[...] = x + _dot(attn_ref[...], wo_ref[...]) + bo_ref[...]


def _swa(x3, positions, norm, w_qkv, b_qkv, sinks, w_out, b_out, *, casts=()):
    bsz, s, d = x3.shape
    rows = min(SWA_ROWS, s)
    d_q = SWA_HEADS * SWA_HEAD_DIM
    d_kv = SWA_KV_HEADS * SWA_HEAD_DIM
    n_qkv = d_q + 2 * d_kv
    invf = (ROPE_THETA ** (-jnp.arange(0, ROPE_DIM, 2, dtype=F32) / ROPE_DIM)).reshape(ROPE_HALF, 1)
    xspec = pl.BlockSpec((None, rows, d), lambda b, i: (b, i, 0))

    def kv_buffer(width):
        return pltpu.VMEM((SWA_KV_HEADS, SWA_BLOCK + rows, width), BF16)

    return _sub_block_call(
        functools.partial(_swa_body, rows=rows),
        grid=(bsz, s // rows),
        semantics=("arbitrary", "arbitrary"),
        in_specs=[
            xspec,
            pl.BlockSpec((None, 1, rows), lambda b, i: (b, 0, i)),
            _resident((1, d)),
            _resident((d, n_qkv)),
            _resident((1, n_qkv)),
            _resident((ROPE_HALF, 1)),
            pl.BlockSpec(memory_space=pltpu.SMEM),
            _resident((d_q, d)),
            _resident((1, d)),
        ],
        out_spec=xspec,
        out_shape=jax.ShapeDtypeStruct((bsz, s, d), F32),
        scratch_shapes=[
            kv_buffer(LANES), kv_buffer(LANES), kv_buffer(2 * LANES), kv_buffer(2 * LANES),
            pltpu.VMEM((SWA_KV_HEADS, rows // SWA_BLOCK, d_q // SWA_KV_HEADS, LANES), BF16),
            pltpu.VMEM((2, 2, d_q // SWA_KV_HEADS, 2 * SWA_BLOCK), BF16),
            pltpu.VMEM((rows, d_q), BF16),
        ],
        name="swa",
        args=(x3, positions.reshape(bsz, 1, s), norm.reshape(1, d), w_qkv, b_qkv.reshape(1, n_qkv),
              invf, sinks, w_out, b_out.reshape(1, d)),
        casts=casts,
    )


def kernel(x, positions, l0_ffn1_norm, l0_ffn1_w_in, l0_ffn1_w_out, l0_mix_norm, l0_gla_w_in, l0_gla_w_gk_up, l0_gla_b_gk, l0_gla_o_norm, l0_gla_w_out, l0_ffn2_norm, l0_ffn2_w_in, l0_ffn2_w_out, l1_ffn1_norm, l1_ffn1_w_in, l1_ffn1_w_out, l1_mix_norm, l1_sgu_w_in, l1_sgu_ln_gain, l1_sgu_ln_bias, l1_sgu_w_s, l1_sgu_b_s, l1_sgu_w_out, l1_ffn2_norm, l1_ffn2_w_in, l1_ffn2_w_out, l2_ffn1_norm, l2_ffn1_w_in, l2_ffn1_w_out, l2_mix_norm, l2_swa_w_qkv, l2_swa_b_qkv, l2_swa_sinks, l2_swa_w_out, l2_swa_b_out, l2_ffn2_norm, l2_ffn2_w_in, l2_ffn2_w_out, l3_ffn1_norm, l3_ffn1_w_in, l3_ffn1_w_out, l3_mix_norm, l3_gla_w_in, l3_gla_w_gk_up, l3_gla_b_gk, l3_gla_o_norm, l3_gla_w_out, l3_ffn2_norm, l3_ffn2_w_in, l3_ffn2_w_out, final_norm):
    bsz, s, d = x.shape

    def ffn(t, norm, w_bf16, casts, final=False):
        y, nxt = _ffn(t.reshape(bsz * s, d), norm, *w_bf16, final_norm, final=final, casts=casts)
        return y.reshape(bsz, s, d), nxt

    def ffn_casts(w_in, w_out):
        return [_whole(w_in), _whole(w_out)]

    w = [l0_ffn1_w_in.astype(BF16), l0_ffn1_w_out.astype(BF16)]
    x, w = ffn(x, l0_ffn1_norm, w, _gla_casts(l0_gla_w_in, l0_gla_w_out))
    x, w = _gla(x, l0_mix_norm, w[0], w[1], l0_gla_w_gk_up, l0_gla_b_gk, l0_gla_o_norm, w[2],
                casts=ffn_casts(l0_ffn2_w_in, l0_ffn2_w_out))
    x, w = ffn(x, l0_ffn2_norm, w, ffn_casts(l1_ffn1_w_in, l1_ffn1_w_out))
    x, w = ffn(x, l1_ffn1_norm, w, ffn_casts(l1_sgu_w_in, l1_sgu_w_out))
    x, w = _sgu(x.reshape(bsz * s, d), l1_mix_norm, w[0], l1_sgu_ln_gain, l1_sgu_ln_bias,
                l1_sgu_w_s, l1_sgu_b_s, w[1], casts=ffn_casts(l1_ffn2_w_in, l1_ffn2_w_out))
    x = x.reshape(bsz, s, d)
    x, w = ffn(x, l1_ffn2_norm, w, ffn_casts(l2_ffn1_w_in, l2_ffn1_w_out))
    x, w = ffn(x, l2_ffn1_norm, w, ffn_casts(l2_swa_w_qkv, l2_swa_w_out))
    x, w = _swa(x, positions, l2_mix_norm, w[0], l2_swa_b_qkv, l2_swa_sinks, w[1], l2_swa_b_out,
                casts=ffn_casts(l2_ffn2_w_in, l2_ffn2_w_out))
    x, w = ffn(x, l2_ffn2_norm, w, ffn_casts(l3_ffn1_w_in, l3_ffn1_w_out))
    x, w = ffn(x, l3_ffn1_norm, w, _gla_casts(l3_gla_w_in, l3_gla_w_out))
    x, w = _gla(x, l3_mix_norm, w[0], w[1], l3_gla_w_gk_up, l3_gla_b_gk, l3_gla_o_norm, w[2],
                casts=ffn_casts(l3_ffn2_w_in, l3_ffn2_w_out))
    x, _ = ffn(x, l3_ffn2_norm, w, [], final=True)
    return x
```

```python
import functools

import jax
import jax.numpy as jnp
from jax import lax
from jax.experimental import pallas as pl
from jax.experimental.pallas import tpu as pltpu

F32 = jnp.float32
BF16 = jnp.bfloat16

NORM_EPS = 1e-5
LANES = 128
VMEM_LIMIT_BYTES = 56 * 1024 * 1024

GLA_HEADS = 4
GLA_HEAD_K = 128
GLA_HEAD_V = 256
GLA_DK = GLA_HEADS * GLA_HEAD_K
GLA_DV = GLA_HEADS * GLA_HEAD_V
GLA_GATE_RANK = 16
GLA_GATE_NORMALIZER = 16.0
GLA_CHUNK = 64
SGU_GROUPS = 8
SGU_CHUNK = 128
SWA_HEADS = 16
SWA_KV_HEADS = 2
SWA_HEAD_DIM = 64
SWA_GROUP = SWA_HEADS // SWA_KV_HEADS
SWA_BLOCK = 128
ROPE_DIM = SWA_HEAD_DIM // 4
ROPE_HALF = ROPE_DIM // 2
ROPE_THETA = 500000.0
MASKED_SCORE = -1e30
LOG2_E = 1.4426950408889634


def _dot(a, b):
    return jnp.dot(a, b, preferred_element_type=F32)


def _dot_nt(a, b):
    return lax.dot_general(a, b, (((1,), (1,)), ((), ())), preferred_element_type=F32)


def _dot_tn(a, b):
    return lax.dot_general(a, b, (((0,), (0,)), ((), ())), preferred_element_type=F32)


def _rms_norm(x, gain):
    return x * lax.rsqrt(jnp.mean(x * x, axis=-1, keepdims=True) + NORM_EPS) * gain


def _silu(x):
    return x * (1.0 / (1.0 + jnp.exp(-x)))


def _resident(shape):
    zeros = (0,) * len(shape)
    return pl.BlockSpec(shape, lambda *_: zeros, pipeline_mode=pl.Buffered(1))


def _compiler_params(semantics):
    return pltpu.CompilerParams(dimension_semantics=semantics,
                                vmem_limit_bytes=VMEM_LIMIT_BYTES)


BF16_SUBLANES = 16


def _whole(w):
    return (w, [(0, w.shape[1], w.shape[1])])


def _cast_rows(n_rows, n_steps):
    for rb in range(BF16_SUBLANES, n_rows + 1, BF16_SUBLANES):
        if n_rows % rb == 0 and n_rows // rb <= n_steps:
            return rb
    raise ValueError("no aligned row block for %d rows in %d steps" % (n_rows, n_steps))


def _cast_blocks(in_refs, out_refs, col_plan):
    outs = iter(out_refs)
    for ref, cols in zip(in_refs, col_plan):
        for lo, hi, width in cols:
            blk = ref[:, lo:hi]
            if width > hi - lo:
                blk = jnp.concatenate([blk, jnp.zeros((blk.shape[0], width - (hi - lo)), F32)], axis=1)
            next(outs)[...] = blk.astype(BF16)


def _sub_block_call(body, *, grid, semantics, in_specs, out_spec, out_shape, scratch_shapes,
                    name, args, casts):
    n_steps = 1
    for g in grid:
        n_steps *= g

    def step_of(*idx):
        step = idx[0]
        for i, g in zip(idx[1:], grid[1:]):
            step = step * g + i
        return step

    cast_in_specs, cast_out_specs, cast_out_shapes, col_plan = [], [], [], []
    for arr, cols in casts:
        n_rows, n_cols = arr.shape
        rb = _cast_rows(n_rows, n_steps)
        n_blocks = n_rows // rb
        stride = n_steps // n_blocks

        def index_map(*idx, stride=stride, n_blocks=n_blocks):
            return (jnp.minimum(step_of(*idx) // stride, n_blocks - 1), 0)

        cast_in_specs.append(pl.BlockSpec((rb, n_cols), index_map))
        col_plan.append(cols)
        for _, _, width in cols:
            cast_out_specs.append(pl.BlockSpec((rb, width), index_map))
            cast_out_shapes.append(jax.ShapeDtypeStruct((n_rows, width), BF16))

    n_in, n_cast_in, n_cast_out = len(in_specs), len(casts), len(cast_out_specs)

    def body_with_casts(*refs):
        main_in = refs[:n_in]
        cast_in = refs[n_in:n_in + n_cast_in]
        out_ref = refs[n_in + n_cast_in]
        cast_out = refs[n_in + n_cast_in + 1:n_in + n_cast_in + 1 + n_cast_out]
        scratch = refs[n_in + n_cast_in + 1 + n_cast_out:]
        _cast_blocks(cast_in, cast_out, col_plan)
        body(*main_in, out_ref, *scratch)

    outs = pl.pallas_call(
        body_with_casts,
        grid=grid,
        in_specs=list(in_specs) + cast_in_specs,
        out_specs=[out_spec] + cast_out_specs,
        out_shape=[out_shape] + cast_out_shapes,
        scratch_shapes=scratch_shapes,
        compiler_params=_compiler_params(semantics),
        name=name,
    )(*args, *[arr for arr, _ in casts])
    return outs[0], list(outs[1:])


FFN_ROWS = 1024
FFN_COLS = 256


def _ffn_body(x_ref, g_ref, wi_ref, wo_ref, fg_ref, o_ref, act_ref, *, d_ff, final):
    x = x_ref[...]
    xn = _rms_norm(x, g_ref[...]).astype(BF16)
    for c in range(d_ff // FFN_COLS):
        lo = c * FFN_COLS
        gate = _dot(xn, wi_ref[:, lo:lo + FFN_COLS])
        up = _dot(xn, wi_ref[:, d_ff + lo:d_ff + lo + FFN_COLS])
        act_ref[:, lo:lo + FFN_COLS] = (_silu(gate) * up).astype(BF16)
    y = x + 0.5 * _dot(act_ref[...], wo_ref[...])
    if final:
        y = _rms_norm(y, fg_ref[...])
    o_ref[...] = y


def _ffn(x2, norm, w_in, w_out, final_gain, *, final, casts=()):
    t, d = x2.shape
    d_ff = w_out.shape[0]
    rows = min(FFN_ROWS, t)
    return _sub_block_call(
        functools.partial(_ffn_body, d_ff=d_ff, final=final),
        grid=(t // rows,),
        semantics=("arbitrary",),
        in_specs=[
            pl.BlockSpec((rows, d), lambda i: (i, 0)),
            _resident((1, d)),
            _resident((d, 2 * d_ff)),
            _resident((d_ff, d)),
            _resident((1, d)),
        ],
        out_spec=pl.BlockSpec((rows, d), lambda i: (i, 0)),
        out_shape=jax.ShapeDtypeStruct((t, d), F32),
        scratch_shapes=[pltpu.VMEM((rows, d_ff), BF16)],
        name="ffn_final" if final else "ffn",
        args=(x2, norm.reshape(1, d), w_in, w_out, final_gain.reshape(1, d)),
        casts=casts,
    )


GLA_ROWS = 1024


def _gla_body(x_ref, g_ref, wq_ref, wgk_ref, wup_ref, bgk_ref, on_ref, wo_ref, o_ref,
              state_ref, qa_ref, ka_ref, qs_ref, ks_ref, v_ref, attn_ref, oacc_ref, gate_ref, *, rows):

    @pl.when(pl.program_id(1) == 0)
    def _():
        state_ref[...] = jnp.zeros_like(state_ref)

    x = x_ref[...]
    xn = _rms_norm(x, g_ref[...]).astype(BF16)
    gk_low = _dot(xn, wgk_ref[...])
    v_ref[...] = _dot(xn, wq_ref[:, 2 * GLA_DK:2 * GLA_DK + GLA_DV]).astype(BF16)
    z = _dot(gk_low.astype(BF16), wup_ref[...]) + bgk_ref[...]
    gate_ref[...] = _silu(_dot(xn, wq_ref[:, 2 * GLA_DK + GLA_DV:2 * GLA_DK + 2 * GLA_DV]))
    log_a = (jnp.minimum(z, 0.0) - jnp.log1p(jnp.exp(-jnp.abs(z)))) * (1.0 / GLA_GATE_NORMALIZER)
    q = _dot(xn, wq_ref[:, 0:GLA_DK]) * (GLA_HEAD_K ** -0.5)
    k = _dot(xn, wq_ref[:, GLA_DK:2 * GLA_DK])

    ri = lax.broadcasted_iota(jnp.int32, (GLA_CHUNK, GLA_CHUNK), 0)
    ci = lax.broadcasted_iota(jnp.int32, (GLA_CHUNK, GLA_CHUNK), 1)
    causal = ri >= ci
    tril = causal.astype(BF16)
    tril3 = jnp.concatenate([tril, tril, tril], axis=1)
    la_hi = log_a.astype(BF16)
    rest = log_a - la_hi.astype(F32)
    la_mid = rest.astype(BF16)
    la_lo = (rest - la_mid.astype(F32)).astype(BF16)

    pair = 2 * GLA_CHUNK
    pair_decays = []
    for p in range(rows // pair):
        r1 = slice(p * pair, p * pair + GLA_CHUNK)
        r2 = slice(p * pair + GLA_CHUNK, (p + 1) * pair)
        b1 = _dot(tril3, jnp.concatenate([la_hi[r1], la_mid[r1], la_lo[r1]], axis=0))
        b2 = _dot(tril3, jnp.concatenate([la_hi[r2], la_mid[r2], la_lo[r2]], axis=0))
        bl1 = b1[GLA_CHUNK - 1:GLA_CHUNK]
        bl2 = b2[GLA_CHUNK - 1:GLA_CHUNK]
        d1 = jnp.exp(bl1)
        d2 = jnp.exp(bl2)
        qd1 = q[r1] * jnp.exp(b1)
        qd2 = q[r2] * jnp.exp(b2)
        ks1 = k[r1] * jnp.exp(bl1 - b1)
        ks2 = k[r2] * jnp.exp(bl2 - b2)
        qa_ref[r1] = (q[r1] * jnp.exp(b1 - bl1)).astype(BF16)
        qa_ref[r2] = qd2.astype(BF16)
        ka_ref[r1] = ks1.astype(BF16)
        ka_ref[r2] = (k[r2] * jnp.exp(-b2)).astype(BF16)
        qs_ref[r1] = qd1.astype(BF16)
        qs_ref[r2] = (qd2 * d1).astype(BF16)
        ks_ref[r1] = (ks1 * d2).astype(BF16)
        ks_ref[r2] = ks2.astype(BF16)
        pair_decays.append(d1 * d2)

    pi = lax.broadcasted_iota(jnp.int32, (pair, pair), 0)
    pj = lax.broadcasted_iota(jnp.int32, (pair, pair), 1)
    pair_causal = pi >= pj
    for p in range(rows // pair):
        rs = slice(p * pair, (p + 1) * pair)
        for h in range(GLA_HEADS):
            ks_ = slice(h * GLA_HEAD_K, (h + 1) * GLA_HEAD_K)
            attn = jnp.where(pair_causal, _dot_nt(qa_ref[rs, ks_], ka_ref[rs, ks_]), 0.0)
            attn_ref[h, rs] = attn.astype(BF16)

    states = [state_ref[h] for h in range(GLA_HEADS)]
    for p in range(rows // pair):
        rs = slice(p * pair, (p + 1) * pair)
        for h in range(GLA_HEADS):
            ks_ = slice(h * GLA_HEAD_K, (h + 1) * GLA_HEAD_K)
            vs_ = slice(h * GLA_HEAD_V, (h + 1) * GLA_HEAD_V)
            v_h = v_ref[rs, vs_]
            oacc_ref[rs, vs_] = (_dot(attn_ref[h, rs], v_h)
                                 + _dot_nt(qs_ref[rs, ks_], states[h].astype(BF16)))
            states[h] = states[h] * pair_decays[p][:, ks_] + _dot_tn(v_h, ks_ref[rs, ks_])
    for h in range(GLA_HEADS):
        state_ref[h] = states[h]

    gain = on_ref[...]
    y = x
    for h in range(GLA_HEADS):
        vs_ = slice(h * GLA_HEAD_V, (h + 1) * GLA_HEAD_V)
        o_h = _rms_norm(oacc_ref[:, vs_], gain) * gate_ref[:, vs_]
        y = y + _dot(o_h.astype(BF16), wo_ref[vs_, :])
    o_ref[...] = y


GLA_N_QKVR = 2 * GLA_DK + 2 * GLA_DV


def _gla_casts(w_in, w_out):
    return [(w_in, [(0, GLA_N_QKVR, GLA_N_QKVR), (GLA_N_QKVR, w_in.shape[1], LANES)]), _whole(w_out)]


def _gla(x3, norm, w_qkvr, w_gk, w_gk_up, b_gk, o_norm, w_out, *, casts=()):
    bsz, s, d = x3.shape
    rows = min(GLA_ROWS, s)
    w_up = jnp.pad(w_gk_up, ((0, LANES - GLA_GATE_RANK), (0, 0))).astype(BF16)
    xspec = pl.BlockSpec((None, rows, d), lambda b, i: (b, i, 0))
    return _sub_block_call(
        functools.partial(_gla_body, rows=rows),
        grid=(bsz, s // rows),
        semantics=("arbitrary", "arbitrary"),
        in_specs=[
            xspec,
            _resident((1, d)),
            _resident((d, GLA_N_QKVR)),
            _resident((d, LANES)),
            _resident((LANES, GLA_DK)),
            _resident((1, GLA_DK)),
            _resident((1, GLA_HEAD_V)),
            _resident((GLA_DV, d)),
        ],
        out_spec=xspec,
        out_shape=jax.ShapeDtypeStruct((bsz, s, d), F32),
        scratch_shapes=[
            pltpu.VMEM((GLA_HEADS, GLA_HEAD_V, GLA_HEAD_K), F32),
            pltpu.VMEM((rows, GLA_DK), BF16),
            pltpu.VMEM((rows, GLA_DK), BF16),
            pltpu.VMEM((rows, GLA_DK), BF16),
            pltpu.VMEM((rows, GLA_DK), BF16),
            pltpu.VMEM((rows, GLA_DV), BF16),
            pltpu.VMEM((GLA_HEADS, rows, 2 * GLA_CHUNK), BF16),
            pltpu.VMEM((rows, GLA_DV), F32),
            pltpu.VMEM((rows, GLA_DV), F32),
        ],
        name="gla",
        args=(x3, norm.reshape(1, d), w_qkvr, w_gk, w_up, b_gk.reshape(1, GLA_DK),
              o_norm.reshape(1, GLA_HEAD_V), w_out),
        casts=casts,
    )


SGU_ROWS = 512
SGU_COLS = 768


def _gelu(x):
    return 0.5 * x * (1.0 + lax.erf(x * (2.0 ** -0.5)))


def _sgu_body(x_ref, g_ref, wi_ref, lng_ref, lnb_ref, ws_ref, bs_ref, wo_ref, o_ref,
              v_ref, uv_ref, *, rows, half):
    x = x_ref[...]
    xn = _rms_norm(x, g_ref[...]).astype(BF16)
    n_col = half // SGU_COLS

    total = jnp.zeros((rows, 1), F32)
    for c in range(n_col):
        lo = c * SGU_COLS
        v_c = _gelu(_dot(xn, wi_ref[:, half + lo:half + lo + SGU_COLS]))
        v_ref[:, lo:lo + SGU_COLS] = v_c
        total = total + jnp.sum(v_c, axis=-1, keepdims=True)
    mu = total * (1.0 / half)
    sq = jnp.zeros((rows, 1), F32)
    for c in range(n_col):
        lo = c * SGU_COLS
        d_c = v_ref[:, lo:lo + SGU_COLS] - mu
        sq = sq + jnp.sum(d_c * d_c, axis=-1, keepdims=True)
    inv = lax.rsqrt(sq * (1.0 / half) + NORM_EPS)

    ri = lax.broadcasted_iota(jnp.int32, (SGU_CHUNK, SGU_CHUNK), 0)
    ci = lax.broadcasted_iota(jnp.int32, (SGU_CHUNK, SGU_CHUNK), 1)
    causal = ri >= ci
    group_dim = half // SGU_GROUPS
    n_chunks = rows // SGU_CHUNK
    for c in range(n_col):
        u_c = _gelu(_dot(xn, wi_ref[:, c * SGU_COLS:(c + 1) * SGU_COLS]))
        for g in range(c * SGU_COLS // group_dim, (c + 1) * SGU_COLS // group_dim):
            lo = g * group_dim
            cols = slice(lo, lo + group_dim)
            vn = ((v_ref[:, cols] - mu) * inv * lng_ref[:, cols] + lnb_ref[:, cols]).astype(BF16)
            w_g = jnp.where(causal, ws_ref[g], 0.0).astype(BF16)
            bias = bs_ref[:, g:g + 1]
            vn_wide = jnp.concatenate(
                [vn[j * SGU_CHUNK:(j + 1) * SGU_CHUNK] for j in range(n_chunks)], axis=1)
            mixed = _dot(w_g, vn_wide) + bias
            for j in range(n_chunks):
                rs = slice(j * SGU_CHUNK, (j + 1) * SGU_CHUNK)
                u_g = u_c[rs, lo - c * SGU_COLS:lo - c * SGU_COLS + group_dim]
                uv_ref[rs, cols] = (u_g * mixed[:, j * group_dim:(j + 1) * group_dim]).astype(BF16)
    o_ref[...] = x + _dot(uv_ref[...], wo_ref[...])


def _sgu(x2, norm, w_in, ln_gain, ln_bias, w_s, b_s, w_out, *, casts=()):
    t, d = x2.shape
    half = w_out.shape[0]
    rows = min(SGU_ROWS, t)
    return _sub_block_call(
        functools.partial(_sgu_body, rows=rows, half=half),
        grid=(t // rows,),
        semantics=("arbitrary",),
        in_specs=[
            pl.BlockSpec((rows, d), lambda i: (i, 0)),
            _resident((1, d)),
            _resident((d, 2 * half)),
            _resident((1, half)),
            _resident((1, half)),
            _resident((SGU_GROUPS, SGU_CHUNK, SGU_CHUNK)),
            _resident((SGU_CHUNK, SGU_GROUPS)),
            _resident((half, d)),
        ],
        out_spec=pl.BlockSpec((rows, d), lambda i: (i, 0)),
        out_shape=jax.ShapeDtypeStruct((t, d), F32),
        scratch_shapes=[
            pltpu.VMEM((rows, half), F32),
            pltpu.VMEM((rows, half), BF16),
        ],
        name="sgu",
        args=(x2, norm.reshape(1, d), w_in, ln_gain.reshape(1, half), ln_bias.reshape(1, half),
              w_s, b_s.T, w_out),
        casts=casts,
    )


SWA_ROWS = 1024


def _swa_body(x_ref, pos_ref, g_ref, wqkv_ref, bqkv_ref, invf_ref, sinks_ref, wo_ref, bo_ref,
              o_ref, klo_ref, khi_ref, vlo_ref, vhi_ref, q_ref, p_ref, attn_ref, *, rows):
    d_q = SWA_HEADS * SWA_HEAD_DIM
    d_kv = SWA_KV_HEADS * SWA_HEAD_DIM
    n_blocks = rows // SWA_BLOCK
    tiles_per_kv = SWA_GROUP * SWA_HEAD_DIM // LANES
    step = pl.program_id(1)
    kv_refs = (klo_ref, khi_ref, vlo_ref, vhi_ref)

    @pl.when(step == 0)
    def _():
        for ref in kv_refs:
            ref[:, 0:SWA_BLOCK] = jnp.zeros((SWA_KV_HEADS, SWA_BLOCK, ref.shape[-1]), BF16)

    @pl.when(step > 0)
    def _():
        for ref in kv_refs:
            ref[:, 0:SWA_BLOCK] = ref[:, rows:rows + SWA_BLOCK]

    x = x_ref[...]
    xn = _rms_norm(x, g_ref[...]).astype(BF16)

    kv = _dot(xn, wqkv_ref[:, d_q:d_q + 2 * d_kv]) + bqkv_ref[:, d_q:d_q + 2 * d_kv]
    tiles = [kv[:, :d_kv]]
    q_cols = 2 * LANES
    for c in range(d_q // q_cols):
        q2 = _dot(xn, wqkv_ref[:, c * q_cols:(c + 1) * q_cols]) + bqkv_ref[:, c * q_cols:(c + 1) * q_cols]
        tiles += [q2[:, t * LANES:(t + 1) * LANES] for t in range(q_cols // LANES)]

    ang = invf_ref[...] * pos_ref[...].astype(F32)
    sel = (lax.broadcasted_iota(jnp.int32, (3 * ROPE_HALF, LANES), 0) % ROPE_HALF
           == lax.broadcasted_iota(jnp.int32, (3 * ROPE_HALF, LANES), 1) % ROPE_HALF).astype(F32)

    def spread(c):
        hi = c.astype(BF16).astype(F32)
        mid = (c - hi).astype(BF16).astype(F32)
        lo = (c - hi - mid).astype(BF16).astype(F32)
        return _dot_tn(jnp.concatenate([hi, mid, lo], axis=0), sel)

    lane = lax.broadcasted_iota(jnp.int32, (1, LANES), 1)
    head_lane = lane % SWA_HEAD_DIM
    sin_t = spread(jnp.sin(ang))
    cos_t = jnp.where(head_lane < ROPE_DIM, spread(jnp.cos(ang)), 1.0)
    sin_s = jnp.where(head_lane < ROPE_HALF, -sin_t, jnp.where(head_lane < ROPE_DIM, sin_t, 0.0))

    src = lax.broadcasted_iota(jnp.int32, (LANES, LANES), 0)
    dst = lax.broadcasted_iota(jnp.int32, (LANES, LANES), 1)
    dst_head_lane = dst % SWA_HEAD_DIM
    swap = (((dst_head_lane < ROPE_HALF) & (src == dst + ROPE_HALF))
            | ((dst_head_lane >= ROPE_HALF) & (dst_head_lane < ROPE_DIM) & (src == dst - ROPE_HALF)))
    swap2 = jnp.concatenate([swap.astype(BF16), swap.astype(BF16)], axis=0)

    def partner_dot(t):
        hi = t.astype(BF16)
        lo = (t - hi.astype(F32)).astype(BF16)
        return _dot(jnp.concatenate([hi, lo], axis=1), swap2)

    partners = [partner_dot(t) for t in tiles]
    roped = [t * cos_t + p * sin_s for t, p in zip(tiles, partners)]

    low_half = lane < SWA_HEAD_DIM
    cur = slice(SWA_BLOCK, SWA_BLOCK + rows)
    for t, lo_ref, hi_ref in ((roped[0], klo_ref, khi_ref), (kv[:, d_kv:], vlo_ref, vhi_ref)):
        swapped = pltpu.roll(t, SWA_HEAD_DIM, 1)
        lo_ref[0, cur, 0:LANES] = jnp.where(low_half, t, 0.0).astype(BF16)
        hi_ref[0, cur, 0:LANES] = jnp.where(low_half, 0.0, swapped).astype(BF16)
        lo_ref[1, cur, 0:LANES] = jnp.where(low_half, swapped, 0.0).astype(BF16)
        hi_ref[1, cur, 0:LANES] = jnp.where(low_half, 0.0, t).astype(BF16)
    ones_lo = jnp.broadcast_to(jnp.where(low_half, 1.0, 0.0), (rows, LANES)).astype(BF16)
    ones_hi = jnp.broadcast_to(jnp.where(low_half, 0.0, 1.0), (rows, LANES)).astype(BF16)
    for h in range(SWA_KV_HEADS):
        vlo_ref[h, cur, LANES:2 * LANES] = ones_lo
        vhi_ref[h, cur, LANES:2 * LANES] = ones_hi

    for tile, q_rot in enumerate(roped[1:]):
        kvh, slot = divmod(tile, tiles_per_kv)
        q_t = (q_rot * (SWA_HEAD_DIM ** -0.5 * LOG2_E)).astype(BF16)
        for j in range(n_blocks):
            q_ref[kvh, j, slot * SWA_BLOCK:(slot + 1) * SWA_BLOCK] = q_t[j * SWA_BLOCK:(j + 1) * SWA_BLOCK]

    qi = lax.broadcasted_iota(jnp.int32, (SWA_BLOCK, 2 * SWA_BLOCK), 0)
    kj = lax.broadcasted_iota(jnp.int32, (SWA_BLOCK, 2 * SWA_BLOCK), 1)
    delta = qi + SWA_BLOCK - kj
    in_window = (delta >= 0) & (delta < SWA_BLOCK)

    items = [(j, kvh) for j in range(n_blocks) for kvh in range(SWA_KV_HEADS)]

    def score_dots(j, kvh):
        band = slice(j * SWA_BLOCK, (j + 2) * SWA_BLOCK)
        qs = q_ref[kvh, j]
        return [_dot_nt(qs, k_ref[kvh, band]) for k_ref in (klo_ref, khi_ref)]

    s_next = score_dots(*items[0])
    for n, (j, kvh) in enumerate(items):
        rs = slice(j * SWA_BLOCK, (j + 1) * SWA_BLOCK)
        band = slice(j * SWA_BLOCK, (j + 2) * SWA_BLOCK)
        if j == 0:
            valid = in_window & (kj >= jnp.where(step > 0, 0, SWA_BLOCK))
        else:
            valid = in_window
        s_cur = s_next
        if n + 1 < len(items):
            s_next = score_dots(*items[n + 1])
        buf = n % 2
        sink_terms = []
        for e in range(2):
            terms_e = []
            for slot in range(tiles_per_kv):
                ts = slice(slot * SWA_BLOCK, (slot + 1) * SWA_BLOCK)
                sink = sinks_ref[(kvh * tiles_per_kv + slot) * 2 + e] * LOG2_E
                scores = jnp.where(valid, s_cur[e][ts], MASKED_SCORE)
                m = jnp.maximum(jnp.max(scores, axis=-1, keepdims=True), sink)
                p_ref[buf, e, ts] = jnp.exp2(scores - m).astype(BF16)
                terms_e.append(jnp.exp2(sink - m))
            sink_terms.append(terms_e)
        out = _dot(p_ref[buf, 0], vlo_ref[kvh, band]) + _dot(p_ref[buf, 1], vhi_ref[kvh, band])
        for slot in range(tiles_per_kv):
            ts = slice(slot * SWA_BLOCK, (slot + 1) * SWA_BLOCK)
            tile = kvh * tiles_per_kv + slot
            denom = out[ts, LANES:2 * LANES] + jnp.where(low_half, sink_terms[0][slot], sink_terms[1][slot])
            attn_ref[rs, tile * LANES:(tile + 1) * LANES] = (out[ts, 0:LANES] / denom).astype(BF16)
    o_ref[...] = x + _dot(attn_ref[...], wo_ref[...]) + bo_ref[...]


def _swa(x3, positions, norm, w_qkv, b_qkv, sinks, w_out, b_out, *, casts=()):
    bsz, s, d = x3.shape
    rows = min(SWA_ROWS, s)
    d_q = SWA_HEADS * SWA_HEAD_DIM
    d_kv = SWA_KV_HEADS * SWA_HEAD_DIM
    n_qkv = d_q + 2 * d_kv
    invf = (ROPE_THETA ** (-jnp.arange(0, ROPE_DIM, 2, dtype=F32) / ROPE_DIM)).reshape(ROPE_HALF, 1)
    xspec = pl.BlockSpec((None, rows, d), lambda b, i: (b, i, 0))

    def kv_buffer(width):
        return pltpu.VMEM((SWA_KV_HEADS, SWA_BLOCK + rows, width), BF16)

    return _sub_block_call(
        functools.partial(_swa_body, rows=rows),
        grid=(bsz, s // rows),
        semantics=("arbitrary", "arbitrary"),
        in_specs=[
            xspec,
            pl.BlockSpec((None, 1, rows), lambda b, i: (b, 0, i)),
            _resident((1, d)),
            _resident((d, n_qkv)),
            _resident((1, n_qkv)),
            _resident((ROPE_HALF, 1)),
            pl.BlockSpec(memory_space=pltpu.SMEM),
            _resident((d_q, d)),
            _resident((1, d)),
        ],
        out_spec=xspec,
        out_shape=jax.ShapeDtypeStruct((bsz, s, d), F32),
        scratch_shapes=[
            kv_buffer(LANES), kv_buffer(LANES), kv_buffer(2 * LANES), kv_buffer(2 * LANES),
            pltpu.VMEM((SWA_KV_HEADS, rows // SWA_BLOCK, d_q // SWA_KV_HEADS, LANES), BF16),
            pltpu.VMEM((2, 2, d_q // SWA_KV_HEADS, 2 * SWA_BLOCK), BF16),
            pltpu.VMEM((rows, d_q), BF16),
        ],
        name="swa",
        args=(x3, positions.reshape(bsz, 1, s), norm.reshape(1, d), w_qkv, b_qkv.reshape(1, n_qkv),
              invf, sinks, w_out, b_out.reshape(1, d)),
        casts=casts,
    )


def kernel(x, positions, l0_ffn1_norm, l0_ffn1_w_in, l0_ffn1_w_out, l0_mix_norm, l0_gla_w_in, l0_gla_w_gk_up, l0_gla_b_gk, l0_gla_o_norm, l0_gla_w_out, l0_ffn2_norm, l0_ffn2_w_in, l0_ffn2_w_out, l1_ffn1_norm, l1_ffn1_w_in, l1_ffn1_w_out, l1_mix_norm, l1_sgu_w_in, l1_sgu_ln_gain, l1_sgu_ln_bias, l1_sgu_w_s, l1_sgu_b_s, l1_sgu_w_out, l1_ffn2_norm, l1_ffn2_w_in, l1_ffn2_w_out, l2_ffn1_norm, l2_ffn1_w_in, l2_ffn1_w_out, l2_mix_norm, l2_swa_w_qkv, l2_swa_b_qkv, l2_swa_sinks, l2_swa_w_out, l2_swa_b_out, l2_ffn2_norm, l2_ffn2_w_in, l2_ffn2_w_out, l3_ffn1_norm, l3_ffn1_w_in, l3_ffn1_w_out, l3_mix_norm, l3_gla_w_in, l3_gla_w_gk_up, l3_gla_b_gk, l3_gla_o_norm, l3_gla_w_out, l3_ffn2_norm, l3_ffn2_w_in, l3_ffn2_w_out, final_norm):
    bsz, s, d = x.shape

    def ffn(t, norm, w_bf16, casts, final=False):
        y, nxt = _ffn(t.reshape(bsz * s, d), norm, *w_bf16, final_norm, final=final, casts=casts)
        return y.reshape(bsz, s, d), nxt

    def ffn_casts(w_in, w_out):
        return [_whole(w_in), _whole(w_out)]

    w = [l0_ffn1_w_in.astype(BF16), l0_ffn1_w_out.astype(BF16)]
    x, w = ffn(x, l0_ffn1_norm, w, _gla_casts(l0_gla_w_in, l0_gla_w_out))
    x, w = _gla(x, l0_mix_norm, w[0], w[1], l0_gla_w_gk_up, l0_gla_b_gk, l0_gla_o_norm, w[2],
                casts=ffn_casts(l0_ffn2_w_in, l0_ffn2_w_out))
    x, w = ffn(x, l0_ffn2_norm, w, ffn_casts(l1_ffn1_w_in, l1_ffn1_w_out))
    x, w = ffn(x, l1_ffn1_norm, w, ffn_casts(l1_sgu_w_in, l1_sgu_w_out))
    x, w = _sgu(x.reshape(bsz * s, d), l1_mix_norm, w[0], l1_sgu_ln_gain, l1_sgu_ln_bias,
                l1_sgu_w_s, l1_sgu_b_s, w[1], casts=ffn_casts(l1_ffn2_w_in, l1_ffn2_w_out))
    x = x.reshape(bsz, s, d)
    x, w = ffn(x, l1_ffn2_norm, w, ffn_casts(l2_ffn1_w_in, l2_ffn1_w_out))
    x, w = ffn(x, l2_ffn1_norm, w, ffn_casts(l2_swa_w_qkv, l2_swa_w_out))
    x, w = _swa(x, positions, l2_mix_norm, w[0], l2_swa_b_qkv, l2_swa_sinks, w[1], l2_swa_b_out,
                casts=ffn_casts(l2_ffn2_w_in, l2_ffn2_w_out))
    x, w = ffn(x, l2_ffn2_norm, w, ffn_casts(l3_ffn1_w_in, l3_ffn1_w_out))
    x, w = ffn(x, l3_ffn1_norm, w, _gla_casts(l3_gla_w_in, l3_gla_w_out))
    x, w = _gla(x, l3_mix_norm, w[0], w[1], l3_gla_w_gk_up, l3_gla_b_gk, l3_gla_o_norm, w[2],
                casts=ffn_casts(l3_ffn2_w_in, l3_ffn2_w_out))
    x, _ = ffn(x, l3_ffn2_norm, w, [], final=True)
    return x
```

```python
import functools

import jax
import jax.numpy as jnp
from jax import lax
from jax.experimental import pallas as pl
from jax.experimental.pallas import tpu as pltpu

F32 = jnp.float32
BF16 = jnp.bfloat16

NORM_EPS = 1e-5
LANES = 128
VMEM_LIMIT_BYTES = 56 * 1024 * 1024

GLA_HEADS = 4
GLA_HEAD_K = 128
GLA_HEAD_V = 256
GLA_DK = GLA_HEADS * GLA_HEAD_K
GLA_DV = GLA_HEADS * GLA_HEAD_V
GLA_GATE_RANK = 16
GLA_GATE_NORMALIZER = 16.0
GLA_CHUNK = 64
SGU_GROUPS = 8
SGU_CHUNK = 128
SWA_HEADS = 16
SWA_KV_HEADS = 2
SWA_HEAD_DIM = 64
SWA_GROUP = SWA_HEADS // SWA_KV_HEADS
SWA_BLOCK = 128
ROPE_DIM = SWA_HEAD_DIM // 4
ROPE_HALF = ROPE_DIM // 2
ROPE_THETA = 500000.0
MASKED_SCORE = -1e30
LOG2_E = 1.4426950408889634


def _dot(a, b):
    return jnp.dot(a, b, preferred_element_type=F32)


def _dot_nt(a, b):
    return lax.dot_general(a, b, (((1,), (1,)), ((), ())), preferred_element_type=F32)


def _dot_tn(a, b):
    return lax.dot_general(a, b, (((0,), (0,)), ((), ())), preferred_element_type=F32)


def _rms_norm(x, gain):
    return x * lax.rsqrt(jnp.mean(x * x, axis=-1, keepdims=True) + NORM_EPS) * gain


def _silu(x):
    return x * (1.0 / (1.0 + jnp.exp(-x)))


def _resident(shape):
    zeros = (0,) * len(shape)
    return pl.BlockSpec(shape, lambda *_: zeros, pipeline_mode=pl.Buffered(1))


def _compiler_params(semantics):
    return pltpu.CompilerParams(dimension_semantics=semantics,
                                vmem_limit_bytes=VMEM_LIMIT_BYTES)


BF16_SUBLANES = 16


def _whole(w):
    return (w, w.shape[1])


def _cast_rows(n_rows, n_steps):
    for rb in range(BF16_SUBLANES, n_rows + 1, BF16_SUBLANES):
        if n_rows % rb == 0 and n_rows // rb <= n_steps:
            return rb
    raise ValueError("no aligned row block for %d rows in %d steps" % (n_rows, n_steps))


def _cast_blocks(in_refs, out_refs):
    for i_ref, o_ref in zip(in_refs, out_refs):
        o_ref[...] = i_ref[...].astype(BF16)


def _sub_block_call(body, *, grid, semantics, in_specs, out_spec, out_shape, scratch_shapes,
                    name, args, casts):
    n_steps = 1
    for g in grid:
        n_steps *= g

    def step_of(*idx):
        step = idx[0]
        for i, g in zip(idx[1:], grid[1:]):
            step = step * g + i
        return step

    cast_specs, cast_out_shapes = [], []
    for arr, n_cols in casts:
        n_rows = arr.shape[0]
        rb = _cast_rows(n_rows, n_steps)
        n_blocks = n_rows // rb
        stride = n_steps // n_blocks

        def index_map(*idx, stride=stride, n_blocks=n_blocks):
            return (jnp.minimum(step_of(*idx) // stride, n_blocks - 1), 0)

        cast_specs.append(pl.BlockSpec((rb, n_cols), index_map))
        cast_out_shapes.append(jax.ShapeDtypeStruct((n_rows, n_cols), BF16))

    n_in, n_cast = len(in_specs), len(casts)

    def body_with_casts(*refs):
        main_in = refs[:n_in]
        cast_in = refs[n_in:n_in + n_cast]
        out_ref = refs[n_in + n_cast]
        cast_out = refs[n_in + n_cast + 1:n_in + 2 * n_cast + 1]
        scratch = refs[n_in + 2 * n_cast + 1:]
        _cast_blocks(cast_in, cast_out)
        body(*main_in, out_ref, *scratch)

    outs = pl.pallas_call(
        body_with_casts,
        grid=grid,
        in_specs=list(in_specs) + cast_specs,
        out_specs=[out_spec] + cast_specs,
        out_shape=[out_shape] + cast_out_shapes,
        scratch_shapes=scratch_shapes,
        compiler_params=_compiler_params(semantics),
        name=name,
    )(*args, *[arr for arr, _ in casts])
    return outs[0], list(outs[1:])


FFN_ROWS = 1024
FFN_COLS = 256


def _ffn_body(x_ref, g_ref, wi_ref, wo_ref, fg_ref, o_ref, act_ref, *, d_ff, final):
    x = x_ref[...]
    xn = _rms_norm(x, g_ref[...]).astype(BF16)
    for c in range(d_ff // FFN_COLS):
        lo = c * FFN_COLS
        gate = _dot(xn, wi_ref[:, lo:lo + FFN_COLS])
        up = _dot(xn, wi_ref[:, d_ff + lo:d_ff + lo + FFN_COLS])
        act_ref[:, lo:lo + FFN_COLS] = (_silu(gate) * up).astype(BF16)
    y = x + 0.5 * _dot(act_ref[...], wo_ref[...])
    if final:
        y = _rms_norm(y, fg_ref[...])
    o_ref[...] = y


def _ffn(x2, norm, w_in, w_out, final_gain, *, final, casts=()):
    t, d = x2.shape
    d_ff = w_out.shape[0]
    rows = min(FFN_ROWS, t)
    return _sub_block_call(
        functools.partial(_ffn_body, d_ff=d_ff, final=final),
        grid=(t // rows,),
        semantics=("arbitrary",),
        in_specs=[
            pl.BlockSpec((rows, d), lambda i: (i, 0)),
            _resident((1, d)),
            _resident((d, 2 * d_ff)),
            _resident((d_ff, d)),
            _resident((1, d)),
        ],
        out_spec=pl.BlockSpec((rows, d), lambda i: (i, 0)),
        out_shape=jax.ShapeDtypeStruct((t, d), F32),
        scratch_shapes=[pltpu.VMEM((rows, d_ff), BF16)],
        name="ffn_final" if final else "ffn",
        args=(x2, norm.reshape(1, d), w_in, w_out, final_gain.reshape(1, d)),
        casts=casts,
    )


GLA_ROWS = 1024


def _gla_body(x_ref, g_ref, wq_ref, wgk_ref, wup_ref, bgk_ref, on_ref, wo_ref, o_ref,
              state_ref, qa_ref, ka_ref, qs_ref, ks_ref, v_ref, attn_ref, oacc_ref, gate_ref, *, rows):

    @pl.when(pl.program_id(1) == 0)
    def _():
        state_ref[...] = jnp.zeros_like(state_ref)

    x = x_ref[...]
    xn = _rms_norm(x, g_ref[...]).astype(BF16)
    gk_low = _dot(xn, wgk_ref[...])
    v_ref[...] = _dot(xn, wq_ref[:, 2 * GLA_DK:2 * GLA_DK + GLA_DV]).astype(BF16)
    z = _dot(gk_low.astype(BF16), wup_ref[...]) + bgk_ref[...]
    gate_ref[...] = _silu(_dot(xn, wq_ref[:, 2 * GLA_DK + GLA_DV:2 * GLA_DK + 2 * GLA_DV]))
    log_a = (jnp.minimum(z, 0.0) - jnp.log1p(jnp.exp(-jnp.abs(z)))) * (1.0 / GLA_GATE_NORMALIZER)
    q = _dot(xn, wq_ref[:, 0:GLA_DK]) * (GLA_HEAD_K ** -0.5)
    k = _dot(xn, wq_ref[:, GLA_DK:2 * GLA_DK])

    ri = lax.broadcasted_iota(jnp.int32, (GLA_CHUNK, GLA_CHUNK), 0)
    ci = lax.broadcasted_iota(jnp.int32, (GLA_CHUNK, GLA_CHUNK), 1)
    causal = ri >= ci
    tril = causal.astype(BF16)
    tril3 = jnp.concatenate([tril, tril, tril], axis=1)
    la_hi = log_a.astype(BF16)
    rest = log_a - la_hi.astype(F32)
    la_mid = rest.astype(BF16)
    la_lo = (rest - la_mid.astype(F32)).astype(BF16)

    pair = 2 * GLA_CHUNK
    pair_decays = []
    for p in range(rows // pair):
        r1 = slice(p * pair, p * pair + GLA_CHUNK)
        r2 = slice(p * pair + GLA_CHUNK, (p + 1) * pair)
        b1 = _dot(tril3, jnp.concatenate([la_hi[r1], la_mid[r1], la_lo[r1]], axis=0))
        b2 = _dot(tril3, jnp.concatenate([la_hi[r2], la_mid[r2], la_lo[r2]], axis=0))
        bl1 = b1[GLA_CHUNK - 1:GLA_CHUNK]
        bl2 = b2[GLA_CHUNK - 1:GLA_CHUNK]
        d1 = jnp.exp(bl1)
        d2 = jnp.exp(bl2)
        qd1 = q[r1] * jnp.exp(b1)
        qd2 = q[r2] * jnp.exp(b2)
        ks1 = k[r1] * jnp.exp(bl1 - b1)
        ks2 = k[r2] * jnp.exp(bl2 - b2)
        qa_ref[r1] = (q[r1] * jnp.exp(b1 - bl1)).astype(BF16)
        qa_ref[r2] = qd2.astype(BF16)
        ka_ref[r1] = ks1.astype(BF16)
        ka_ref[r2] = (k[r2] * jnp.exp(-b2)).astype(BF16)
        qs_ref[r1] = qd1.astype(BF16)
        qs_ref[r2] = (qd2 * d1).astype(BF16)
        ks_ref[r1] = (ks1 * d2).astype(BF16)
        ks_ref[r2] = ks2.astype(BF16)
        pair_decays.append(d1 * d2)

    pi = lax.broadcasted_iota(jnp.int32, (pair, pair), 0)
    pj = lax.broadcasted_iota(jnp.int32, (pair, pair), 1)
    pair_causal = pi >= pj
    for p in range(rows // pair):
        rs = slice(p * pair, (p + 1) * pair)
        for h in range(GLA_HEADS):
            ks_ = slice(h * GLA_HEAD_K, (h + 1) * GLA_HEAD_K)
            attn = jnp.where(pair_causal, _dot_nt(qa_ref[rs, ks_], ka_ref[rs, ks_]), 0.0)
            attn_ref[h, rs] = attn.astype(BF16)

    states = [state_ref[h] for h in range(GLA_HEADS)]
    for p in range(rows // pair):
        rs = slice(p * pair, (p + 1) * pair)
        for h in range(GLA_HEADS):
            ks_ = slice(h * GLA_HEAD_K, (h + 1) * GLA_HEAD_K)
            vs_ = slice(h * GLA_HEAD_V, (h + 1) * GLA_HEAD_V)
            v_h = v_ref[rs, vs_]
            oacc_ref[rs, vs_] = (_dot(attn_ref[h, rs], v_h)
                                 + _dot_nt(qs_ref[rs, ks_], states[h].astype(BF16)))
            states[h] = states[h] * pair_decays[p][:, ks_] + _dot_tn(v_h, ks_ref[rs, ks_])
    for h in range(GLA_HEADS):
        state_ref[h] = states[h]

    gain = on_ref[...]
    y = x
    for h in range(GLA_HEADS):
        vs_ = slice(h * GLA_HEAD_V, (h + 1) * GLA_HEAD_V)
        o_h = _rms_norm(oacc_ref[:, vs_], gain) * gate_ref[:, vs_]
        y = y + _dot(o_h.astype(BF16), wo_ref[vs_, :])
    o_ref[...] = y


GLA_N_QKVR = 2 * GLA_DK + 2 * GLA_DV


def _gla_casts(w_in, w_out):
    return [(w_in, GLA_N_QKVR), _whole(w_out)]


def _gla(x3, norm, w_qkvr, w_in, w_gk_up, b_gk, o_norm, w_out, *, casts=()):
    bsz, s, d = x3.shape
    rows = min(GLA_ROWS, s)
    w_gk = jnp.pad(w_in[:, GLA_N_QKVR:], ((0, 0), (0, LANES - GLA_GATE_RANK))).astype(BF16)
    w_up = jnp.pad(w_gk_up, ((0, LANES - GLA_GATE_RANK), (0, 0))).astype(BF16)
    xspec = pl.BlockSpec((None, rows, d), lambda b, i: (b, i, 0))
    return _sub_block_call(
        functools.partial(_gla_body, rows=rows),
        grid=(bsz, s // rows),
        semantics=("arbitrary", "arbitrary"),
        in_specs=[
            xspec,
            _resident((1, d)),
            _resident((d, GLA_N_QKVR)),
            _resident((d, LANES)),
            _resident((LANES, GLA_DK)),
            _resident((1, GLA_DK)),
            _resident((1, GLA_HEAD_V)),
            _resident((GLA_DV, d)),
        ],
        out_spec=xspec,
        out_shape=jax.ShapeDtypeStruct((bsz, s, d), F32),
        scratch_shapes=[
            pltpu.VMEM((GLA_HEADS, GLA_HEAD_V, GLA_HEAD_K), F32),
            pltpu.VMEM((rows, GLA_DK), BF16),
            pltpu.VMEM((rows, GLA_DK), BF16),
            pltpu.VMEM((rows, GLA_DK), BF16),
            pltpu.VMEM((rows, GLA_DK), BF16),
            pltpu.VMEM((rows, GLA_DV), BF16),
            pltpu.VMEM((GLA_HEADS, rows, 2 * GLA_CHUNK), BF16),
            pltpu.VMEM((rows, GLA_DV), F32),
            pltpu.VMEM((rows, GLA_DV), F32),
        ],
        name="gla",
        args=(x3, norm.reshape(1, d), w_qkvr, w_gk, w_up, b_gk.reshape(1, GLA_DK),
              o_norm.reshape(1, GLA_HEAD_V), w_out),
        casts=casts,
    )


SGU_ROWS = 512
SGU_COLS = 768


def _gelu(x):
    return 0.5 * x * (1.0 + lax.erf(x * (2.0 ** -0.5)))


def _sgu_body(x_ref, g_ref, wi_ref, lng_ref, lnb_ref, ws_ref, bs_ref, wo_ref, o_ref,
              v_ref, uv_ref, *, rows, half):
    x = x_ref[...]
    xn = _rms_norm(x, g_ref[...]).astype(BF16)
    n_col = half // SGU_COLS

    total = jnp.zeros((rows, 1), F32)
    for c in range(n_col):
        lo = c * SGU_COLS
        v_c = _gelu(_dot(xn, wi_ref[:, half + lo:half + lo + SGU_COLS]))
        v_ref[:, lo:lo + SGU_COLS] = v_c
        total = total + jnp.sum(v_c, axis=-1, keepdims=True)
    mu = total * (1.0 / half)
    sq = jnp.zeros((rows, 1), F32)
    for c in range(n_col):
        lo = c * SGU_COLS
        d_c = v_ref[:, lo:lo + SGU_COLS] - mu
        sq = sq + jnp.sum(d_c * d_c, axis=-1, keepdims=True)
    inv = lax.rsqrt(sq * (1.0 / half) + NORM_EPS)

    ri = lax.broadcasted_iota(jnp.int32, (SGU_CHUNK, SGU_CHUNK), 0)
    ci = lax.broadcasted_iota(jnp.int32, (SGU_CHUNK, SGU_CHUNK), 1)
    causal = ri >= ci
    group_dim = half // SGU_GROUPS
    n_chunks = rows // SGU_CHUNK
    for c in range(n_col):
        u_c = _gelu(_dot(xn, wi_ref[:, c * SGU_COLS:(c + 1) * SGU_COLS]))
        for g in range(c * SGU_COLS // group_dim, (c + 1) * SGU_COLS // group_dim):
            lo = g * group_dim
            cols = slice(lo, lo + group_dim)
            vn = ((v_ref[:, cols] - mu) * inv * lng_ref[:, cols] + lnb_ref[:, cols]).astype(BF16)
            w_g = jnp.where(causal, ws_ref[g], 0.0).astype(BF16)
            bias = bs_ref[:, g:g + 1]
            vn_wide = jnp.concatenate(
                [vn[j * SGU_CHUNK:(j + 1) * SGU_CHUNK] for j in range(n_chunks)], axis=1)
            mixed = _dot(w_g, vn_wide) + bias
            for j in range(n_chunks):
                rs = slice(j * SGU_CHUNK, (j + 1) * SGU_CHUNK)
                u_g = u_c[rs, lo - c * SGU_COLS:lo - c * SGU_COLS + group_dim]
                uv_ref[rs, cols] = (u_g * mixed[:, j * group_dim:(j + 1) * group_dim]).astype(BF16)
    o_ref[...] = x + _dot(uv_ref[...], wo_ref[...])


def _sgu(x2, norm, w_in, ln_gain, ln_bias, w_s, b_s, w_out, *, casts=()):
    t, d = x2.shape
    half = w_out.shape[0]
    rows = min(SGU_ROWS, t)
    return _sub_block_call(
        functools.partial(_sgu_body, rows=rows, half=half),
        grid=(t // rows,),
        semantics=("arbitrary",),
        in_specs=[
            pl.BlockSpec((rows, d), lambda i: (i, 0)),
            _resident((1, d)),
            _resident((d, 2 * half)),
            _resident((1, half)),
            _resident((1, half)),
            _resident((SGU_GROUPS, SGU_CHUNK, SGU_CHUNK)),
            _resident((SGU_CHUNK, SGU_GROUPS)),
            _resident((half, d)),
        ],
        out_spec=pl.BlockSpec((rows, d), lambda i: (i, 0)),
        out_shape=jax.ShapeDtypeStruct((t, d), F32),
        scratch_shapes=[
            pltpu.VMEM((rows, half), F32),
            pltpu.VMEM((rows, half), BF16),
        ],
        name="sgu",
        args=(x2, norm.reshape(1, d), w_in, ln_gain.reshape(1, half), ln_bias.reshape(1, half),
              w_s, b_s.T, w_out),
        casts=casts,
    )


SWA_ROWS = 1024


def _swa_body(x_ref, pos_ref, g_ref, wqkv_ref, bqkv_ref, invf_ref, sinks_ref, wo_ref, bo_ref,
              o_ref, klo_ref, khi_ref, vlo_ref, vhi_ref, q_ref, p_ref, attn_ref, *, rows):
    d_q = SWA_HEADS * SWA_HEAD_DIM
    d_kv = SWA_KV_HEADS * SWA_HEAD_DIM
    n_blocks = rows // SWA_BLOCK
    tiles_per_kv = SWA_GROUP * SWA_HEAD_DIM // LANES
    step = pl.program_id(1)
    kv_refs = (klo_ref, khi_ref, vlo_ref, vhi_ref)

    @pl.when(step == 0)
    def _():
        for ref in kv_refs:
            ref[:, 0:SWA_BLOCK] = jnp.zeros((SWA_KV_HEADS, SWA_BLOCK, ref.shape[-1]), BF16)

    @pl.when(step > 0)
    def _():
        for ref in kv_refs:
            ref[:, 0:SWA_BLOCK] = ref[:, rows:rows + SWA_BLOCK]

    x = x_ref[...]
    xn = _rms_norm(x, g_ref[...]).astype(BF16)

    kv = _dot(xn, wqkv_ref[:, d_q:d_q + 2 * d_kv]) + bqkv_ref[:, d_q:d_q + 2 * d_kv]
    tiles = [kv[:, :d_kv]]
    q_cols = 2 * LANES
    for c in range(d_q // q_cols):
        q2 = _dot(xn, wqkv_ref[:, c * q_cols:(c + 1) * q_cols]) + bqkv_ref[:, c * q_cols:(c + 1) * q_cols]
        tiles += [q2[:, t * LANES:(t + 1) * LANES] for t in range(q_cols // LANES)]

    ang = invf_ref[...] * pos_ref[...].astype(F32)
    sel = (lax.broadcasted_iota(jnp.int32, (3 * ROPE_HALF, LANES), 0) % ROPE_HALF
           == lax.broadcasted_iota(jnp.int32, (3 * ROPE_HALF, LANES), 1) % ROPE_HALF).astype(F32)

    def spread(c):
        hi = c.astype(BF16).astype(F32)
        mid = (c - hi).astype(BF16).astype(F32)
        lo = (c - hi - mid).astype(BF16).astype(F32)
        return _dot_tn(jnp.concatenate([hi, mid, lo], axis=0), sel)

    lane = lax.broadcasted_iota(jnp.int32, (1, LANES), 1)
    head_lane = lane % SWA_HEAD_DIM
    sin_t = spread(jnp.sin(ang))
    cos_t = jnp.where(head_lane < ROPE_DIM, spread(jnp.cos(ang)), 1.0)
    sin_s = jnp.where(head_lane < ROPE_HALF, -sin_t, jnp.where(head_lane < ROPE_DIM, sin_t, 0.0))

    src = lax.broadcasted_iota(jnp.int32, (LANES, LANES), 0)
    dst = lax.broadcasted_iota(jnp.int32, (LANES, LANES), 1)
    dst_head_lane = dst % SWA_HEAD_DIM
    swap = (((dst_head_lane < ROPE_HALF) & (src == dst + ROPE_HALF))
            | ((dst_head_lane >= ROPE_HALF) & (dst_head_lane < ROPE_DIM) & (src == dst - ROPE_HALF)))
    swap2 = jnp.concatenate([swap.astype(BF16), swap.astype(BF16)], axis=0)

    def partner_dot(t):
        hi = t.astype(BF16)
        lo = (t - hi.astype(F32)).astype(BF16)
        return _dot(jnp.concatenate([hi, lo], axis=1), swap2)

    partners = [partner_dot(t) for t in tiles]
    roped = [t * cos_t + p * sin_s for t, p in zip(tiles, partners)]

    low_half = lane < SWA_HEAD_DIM
    cur = slice(SWA_BLOCK, SWA_BLOCK + rows)
    for t, lo_ref, hi_ref in ((roped[0], klo_ref, khi_ref), (kv[:, d_kv:], vlo_ref, vhi_ref)):
        swapped = pltpu.roll(t, SWA_HEAD_DIM, 1)
        lo_ref[0, cur, 0:LANES] = jnp.where(low_half, t, 0.0).astype(BF16)
        hi_ref[0, cur, 0:LANES] = jnp.where(low_half, 0.0, swapped).astype(BF16)
        lo_ref[1, cur, 0:LANES] = jnp.where(low_half, swapped, 0.0).astype(BF16)
        hi_ref[1, cur, 0:LANES] = jnp.where(low_half, 0.0, t).astype(BF16)
    ones_lo = jnp.broadcast_to(jnp.where(low_half, 1.0, 0.0), (rows, LANES)).astype(BF16)
    ones_hi = jnp.broadcast_to(jnp.where(low_half, 0.0, 1.0), (rows, LANES)).astype(BF16)
    for h in range(SWA_KV_HEADS):
        vlo_ref[h, cur, LANES:2 * LANES] = ones_lo
        vhi_ref[h, cur, LANES:2 * LANES] = ones_hi

    for tile, q_rot in enumerate(roped[1:]):
        kvh, slot = divmod(tile, tiles_per_kv)
        q_t = (q_rot * (SWA_HEAD_DIM ** -0.5 * LOG2_E)).astype(BF16)
        for j in range(n_blocks):
            q_ref[kvh, j, slot * SWA_BLOCK:(slot + 1) * SWA_BLOCK] = q_t[j * SWA_BLOCK:(j + 1) * SWA_BLOCK]

    qi = lax.broadcasted_iota(jnp.int32, (SWA_BLOCK, 2 * SWA_BLOCK), 0)
    kj = lax.broadcasted_iota(jnp.int32, (SWA_BLOCK, 2 * SWA_BLOCK), 1)
    delta = qi + SWA_BLOCK - kj
    in_window = (delta >= 0) & (delta < SWA_BLOCK)

    items = [(j, kvh) for j in range(n_blocks) for kvh in range(SWA_KV_HEADS)]

    def score_dots(j, kvh):
        band = slice(j * SWA_BLOCK, (j + 2) * SWA_BLOCK)
        qs = q_ref[kvh, j]
        return [_dot_nt(qs, k_ref[kvh, band]) for k_ref in (klo_ref, khi_ref)]

    s_next = score_dots(*items[0])
    for n, (j, kvh) in enumerate(items):
        rs = slice(j * SWA_BLOCK, (j + 1) * SWA_BLOCK)
        band = slice(j * SWA_BLOCK, (j + 2) * SWA_BLOCK)
        if j == 0:
            valid = in_window & (kj >= jnp.where(step > 0, 0, SWA_BLOCK))
        else:
            valid = in_window
        s_cur = s_next
        if n + 1 < len(items):
            s_next = score_dots(*items[n + 1])
        buf = n % 2
        sink_terms = []
        for e in range(2):
            terms_e = []
            for slot in range(tiles_per_kv):
                ts = slice(slot * SWA_BLOCK, (slot + 1) * SWA_BLOCK)
                sink = sinks_ref[(kvh * tiles_per_kv + slot) * 2 + e] * LOG2_E
                scores = jnp.where(valid, s_cur[e][ts], MASKED_SCORE)
                m = jnp.maximum(jnp.max(scores, axis=-1, keepdims=True), sink)
                p_ref[buf, e, ts] = jnp.exp2(scores - m).astype(BF16)
                terms_e.append(jnp.exp2(sink - m))
            sink_terms.append(terms_e)
        out = _dot(p_ref[buf, 0], vlo_ref[kvh, band]) + _dot(p_ref[buf, 1], vhi_ref[kvh, band])
        for slot in range(tiles_per_kv):
            ts = slice(slot * SWA_BLOCK, (slot + 1) * SWA_BLOCK)
            tile = kvh * tiles_per_kv + slot
            denom = out[ts, LANES:2 * LANES] + jnp.where(low_half, sink_terms[0][slot], sink_terms[1][slot])
            attn_ref[rs, tile * LANES:(tile + 1) * LANES] = (out[ts, 0:LANES] / denom).astype(BF16)
    o_ref[...] = x + _dot(attn_ref[...], wo_ref[...]) + bo_ref[...]


def _swa(x3, positions, norm, w_qkv, b_qkv, sinks, w_out, b_out, *, casts=()):
    bsz, s, d = x3.shape
    rows = min(SWA_ROWS, s)
    d_q = SWA_HEADS * SWA_HEAD_DIM
    d_kv = SWA_KV_HEADS * SWA_HEAD_DIM
    n_qkv = d_q + 2 * d_kv
    invf = (ROPE_THETA ** (-jnp.arange(0, ROPE_DIM, 2, dtype=F32) / ROPE_DIM)).reshape(ROPE_HALF, 1)
    xspec = pl.BlockSpec((None, rows, d), lambda b, i: (b, i, 0))

    def kv_buffer(width):
        return pltpu.VMEM((SWA_KV_HEADS, SWA_BLOCK + rows, width), BF16)

    return _sub_block_call(
        functools.partial(_swa_body, rows=rows),
        grid=(bsz, s // rows),
        semantics=("arbitrary", "arbitrary"),
        in_specs=[
            xspec,
            pl.BlockSpec((None, 1, rows), lambda b, i: (b, 0, i)),
            _resident((1, d)),
            _resident((d, n_qkv)),
            _resident((1, n_qkv)),
            _resident((ROPE_HALF, 1)),
            pl.BlockSpec(memory_space=pltpu.SMEM),
            _resident((d_q, d)),
            _resident((1, d)),
        ],
        out_spec=xspec,
        out_shape=jax.ShapeDtypeStruct((bsz, s, d), F32),
        scratch_shapes=[
            kv_buffer(LANES), kv_buffer(LANES), kv_buffer(2 * LANES), kv_buffer(2 * LANES),
            pltpu.VMEM((SWA_KV_HEADS, rows // SWA_BLOCK, d_q // SWA_KV_HEADS, LANES), BF16),
            pltpu.VMEM((2, 2, d_q // SWA_KV_HEADS, 2 * SWA_BLOCK), BF16),
            pltpu.VMEM((rows, d_q), BF16),
        ],
        name="swa",
        args=(x3, positions.reshape(bsz, 1, s), norm.reshape(1, d), w_qkv, b_qkv.reshape(1, n_qkv),
              invf, sinks, w_out, b_out.reshape(1, d)),
        casts=casts,
    )


def kernel(x, positions, l0_ffn1_norm, l0_ffn1_w_in, l0_ffn1_w_out, l0_mix_norm, l0_gla_w_in, l0_gla_w_gk_up, l0_gla_b_gk, l0_gla_o_norm, l0_gla_w_out, l0_ffn2_norm, l0_ffn2_w_in, l0_ffn2_w_out, l1_ffn1_norm, l1_ffn1_w_in, l1_ffn1_w_out, l1_mix_norm, l1_sgu_w_in, l1_sgu_ln_gain, l1_sgu_ln_bias, l1_sgu_w_s, l1_sgu_b_s, l1_sgu_w_out, l1_ffn2_norm, l1_ffn2_w_in, l1_ffn2_w_out, l2_ffn1_norm, l2_ffn1_w_in, l2_ffn1_w_out, l2_mix_norm, l2_swa_w_qkv, l2_swa_b_qkv, l2_swa_sinks, l2_swa_w_out, l2_swa_b_out, l2_ffn2_norm, l2_ffn2_w_in, l2_ffn2_w_out, l3_ffn1_norm, l3_ffn1_w_in, l3_ffn1_w_out, l3_mix_norm, l3_gla_w_in, l3_gla_w_gk_up, l3_gla_b_gk, l3_gla_o_norm, l3_gla_w_out, l3_ffn2_norm, l3_ffn2_w_in, l3_ffn2_w_out, final_norm):
    bsz, s, d = x.shape

    def ffn(t, norm, w_bf16, casts, final=False):
        y, nxt = _ffn(t.reshape(bsz * s, d), norm, *w_bf16, final_norm, final=final, casts=casts)
        return y.reshape(bsz, s, d), nxt

    def ffn_casts(w_in, w_out):
        return [_whole(w_in), _whole(w_out)]

    w = [l0_ffn1_w_in.astype(BF16), l0_ffn1_w_out.astype(BF16)]
    x, w = ffn(x, l0_ffn1_norm, w, _gla_casts(l0_gla_w_in, l0_gla_w_out))
    x, w = _gla(x, l0_mix_norm, w[0], l0_gla_w_in, l0_gla_w_gk_up, l0_gla_b_gk, l0_gla_o_norm, w[1],
                casts=ffn_casts(l0_ffn2_w_in, l0_ffn2_w_out))
    x, w = ffn(x, l0_ffn2_norm, w, ffn_casts(l1_ffn1_w_in, l1_ffn1_w_out))
    x, w = ffn(x, l1_ffn1_norm, w, ffn_casts(l1_sgu_w_in, l1_sgu_w_out))
    x, w = _sgu(x.reshape(bsz * s, d), l1_mix_norm, w[0], l1_sgu_ln_gain, l1_sgu_ln_bias,
                l1_sgu_w_s, l1_sgu_b_s, w[1], casts=ffn_casts(l1_ffn2_w_in, l1_ffn2_w_out))
    x = x.reshape(bsz, s, d)
    x, w = ffn(x, l1_ffn2_norm, w, ffn_casts(l2_ffn1_w_in, l2_ffn1_w_out))
    x, w = ffn(x, l2_ffn1_norm, w, ffn_casts(l2_swa_w_qkv, l2_swa_w_out))
    x, w = _swa(x, positions, l2_mix_norm, w[0], l2_swa_b_qkv, l2_swa_sinks, w[1], l2_swa_b_out,
                casts=ffn_casts(l2_ffn2_w_in, l2_ffn2_w_out))
    x, w = ffn(x, l2_ffn2_norm, w, ffn_casts(l3_ffn1_w_in, l3_ffn1_w_out))
    x, w = ffn(x, l3_ffn1_norm, w, _gla_casts(l3_gla_w_in, l3_gla_w_out))
    x, w = _gla(x, l3_mix_norm, w[0], l3_gla_w_in, l3_gla_w_gk_up, l3_gla_b_gk, l3_gla_o_norm, w[1],
                casts=ffn_casts(l3_ffn2_w_in, l3_ffn2_w_out))
    x, _ = ffn(x, l3_ffn2_norm, w, [], final=True)
    return x
```

```python
import functools

import jax
import jax.numpy as jnp
from jax import lax
from jax.experimental import pallas as pl
from jax.experimental.pallas import tpu as pltpu

F32 = jnp.float32
BF16 = jnp.bfloat16

NORM_EPS = 1e-5
LANES = 128
VMEM_LIMIT_BYTES = 58 * 1024 * 1024

GLA_HEADS = 4
GLA_HEAD_K = 128
GLA_HEAD_V = 256
GLA_DK = GLA_HEADS * GLA_HEAD_K
GLA_DV = GLA_HEADS * GLA_HEAD_V
GLA_GATE_RANK = 16
GLA_GATE_NORMALIZER = 16.0
GLA_CHUNK = 64
SGU_GROUPS = 8
SGU_CHUNK = 128
SWA_HEADS = 16
SWA_KV_HEADS = 2
SWA_HEAD_DIM = 64
SWA_GROUP = SWA_HEADS // SWA_KV_HEADS
SWA_BLOCK = 128
ROPE_DIM = SWA_HEAD_DIM // 4
ROPE_HALF = ROPE_DIM // 2
ROPE_THETA = 500000.0
MASKED_SCORE = -1e30
LOG2_E = 1.4426950408889634


def _dot(a, b):
    return jnp.dot(a, b, preferred_element_type=F32)


def _dot_nt(a, b):
    return lax.dot_general(a, b, (((1,), (1,)), ((), ())), preferred_element_type=F32)


def _dot_tn(a, b):
    return lax.dot_general(a, b, (((0,), (0,)), ((), ())), preferred_element_type=F32)


def _rms_norm(x, gain):
    return x * lax.rsqrt(jnp.mean(x * x, axis=-1, keepdims=True) + NORM_EPS) * gain


def _silu(x):
    return x * (1.0 / (1.0 + jnp.exp(-x)))


def _resident(shape):
    zeros = (0,) * len(shape)
    return pl.BlockSpec(shape, lambda *_: zeros, pipeline_mode=pl.Buffered(1))


def _compiler_params(semantics):
    return pltpu.CompilerParams(dimension_semantics=semantics,
                                vmem_limit_bytes=VMEM_LIMIT_BYTES)


BF16_SUBLANES = 16


def _whole(w):
    return (w, w.shape[1])


def _cast_rows(n_rows, n_steps):
    for rb in range(BF16_SUBLANES, n_rows + 1, BF16_SUBLANES):
        if n_rows % rb == 0 and n_rows // rb <= n_steps:
            return rb
    raise ValueError("no aligned row block for %d rows in %d steps" % (n_rows, n_steps))


def _cast_blocks(in_refs, out_refs):
    for i_ref, o_ref in zip(in_refs, out_refs):
        o_ref[...] = i_ref[...].astype(BF16)


def _sub_block_call(body, *, grid, semantics, in_specs, out_spec, out_shape, scratch_shapes,
                    name, args, casts):
    n_steps = 1
    for g in grid:
        n_steps *= g

    def step_of(*idx):
        step = idx[0]
        for i, g in zip(idx[1:], grid[1:]):
            step = step * g + i
        return step

    cast_specs, cast_out_shapes = [], []
    for arr, n_cols in casts:
        n_rows = arr.shape[0]
        rb = _cast_rows(n_rows, n_steps)
        n_blocks = n_rows // rb
        stride = n_steps // n_blocks

        def index_map(*idx, stride=stride, n_blocks=n_blocks):
            return (jnp.minimum(step_of(*idx) // stride, n_blocks - 1), 0)

        cast_specs.append(pl.BlockSpec((rb, n_cols), index_map))
        cast_out_shapes.append(jax.ShapeDtypeStruct((n_rows, n_cols), BF16))

    n_in, n_cast = len(in_specs), len(casts)

    def body_with_casts(*refs):
        main_in = refs[:n_in]
        cast_in = refs[n_in:n_in + n_cast]
        out_ref = refs[n_in + n_cast]
        cast_out = refs[n_in + n_cast + 1:n_in + 2 * n_cast + 1]
        scratch = refs[n_in + 2 * n_cast + 1:]
        _cast_blocks(cast_in, cast_out)
        body(*main_in, out_ref, *scratch)

    outs = pl.pallas_call(
        body_with_casts,
        grid=grid,
        in_specs=list(in_specs) + cast_specs,
        out_specs=[out_spec] + cast_specs,
        out_shape=[out_shape] + cast_out_shapes,
        scratch_shapes=scratch_shapes,
        compiler_params=_compiler_params(semantics),
        name=name,
    )(*args, *[arr for arr, _ in casts])
    return outs[0], list(outs[1:])


FFN_ROWS = 1024
FFN_COLS = 256


def _ffn_body(x_ref, g_ref, wi_ref, wo_ref, fg_ref, o_ref, act_ref, *, d_ff, final):
    x = x_ref[...]
    xn = _rms_norm(x, g_ref[...]).astype(BF16)
    for c in range(d_ff // FFN_COLS):
        lo = c * FFN_COLS
        gate = _dot(xn, wi_ref[:, lo:lo + FFN_COLS])
        up = _dot(xn, wi_ref[:, d_ff + lo:d_ff + lo + FFN_COLS])
        act_ref[:, lo:lo + FFN_COLS] = (_silu(gate) * up).astype(BF16)
    y = x + 0.5 * _dot(act_ref[...], wo_ref[...])
    if final:
        y = _rms_norm(y, fg_ref[...])
    o_ref[...] = y


def _ffn(x2, norm, w_in, w_out, final_gain, *, final, casts=()):
    t, d = x2.shape
    d_ff = w_out.shape[0]
    rows = min(FFN_ROWS, t)
    return _sub_block_call(
        functools.partial(_ffn_body, d_ff=d_ff, final=final),
        grid=(t // rows,),
        semantics=("arbitrary",),
        in_specs=[
            pl.BlockSpec((rows, d), lambda i: (i, 0)),
            _resident((1, d)),
            _resident((d, 2 * d_ff)),
            _resident((d_ff, d)),
            _resident((1, d)),
        ],
        out_spec=pl.BlockSpec((rows, d), lambda i: (i, 0)),
        out_shape=jax.ShapeDtypeStruct((t, d), F32),
        scratch_shapes=[pltpu.VMEM((rows, d_ff), BF16)],
        name="ffn_final" if final else "ffn",
        args=(x2, norm.reshape(1, d), w_in, w_out, final_gain.reshape(1, d)),
        casts=casts,
    )


GLA_ROWS = 1024


def _gla_body(x_ref, g_ref, wq_ref, wgk_ref, wup_ref, bgk_ref, on_ref, wo_ref, o_ref,
              state_ref, qa_ref, ka_ref, qs_ref, ks_ref, v_ref, attn_ref, oacc_ref, gate_ref, *, rows):

    @pl.when(pl.program_id(1) == 0)
    def _():
        state_ref[...] = jnp.zeros_like(state_ref)

    x = x_ref[...]
    xn = _rms_norm(x, g_ref[...]).astype(BF16)
    gk_low = _dot(xn, wgk_ref[...])
    v_ref[...] = _dot(xn, wq_ref[:, 2 * GLA_DK:2 * GLA_DK + GLA_DV]).astype(BF16)
    z = _dot(gk_low.astype(BF16), wup_ref[...]) + bgk_ref[...]
    gate_ref[...] = _silu(_dot(xn, wq_ref[:, 2 * GLA_DK + GLA_DV:2 * GLA_DK + 2 * GLA_DV]))
    log_a = (jnp.minimum(z, 0.0) - jnp.log1p(jnp.exp(-jnp.abs(z)))) * (1.0 / GLA_GATE_NORMALIZER)
    q = _dot(xn, wq_ref[:, 0:GLA_DK]) * (GLA_HEAD_K ** -0.5)
    k = _dot(xn, wq_ref[:, GLA_DK:2 * GLA_DK])

    ri = lax.broadcasted_iota(jnp.int32, (GLA_CHUNK, GLA_CHUNK), 0)
    ci = lax.broadcasted_iota(jnp.int32, (GLA_CHUNK, GLA_CHUNK), 1)
    causal = ri >= ci
    tril = causal.astype(BF16)
    tril3 = jnp.concatenate([tril, tril, tril], axis=1)
    la_hi = log_a.astype(BF16)
    rest = log_a - la_hi.astype(F32)
    la_mid = rest.astype(BF16)
    la_lo = (rest - la_mid.astype(F32)).astype(BF16)

    pair = 2 * GLA_CHUNK
    pair_decays = []
    for p in range(rows // pair):
        r1 = slice(p * pair, p * pair + GLA_CHUNK)
        r2 = slice(p * pair + GLA_CHUNK, (p + 1) * pair)
        b1 = _dot(tril3, jnp.concatenate([la_hi[r1], la_mid[r1], la_lo[r1]], axis=0))
        b2 = _dot(tril3, jnp.concatenate([la_hi[r2], la_mid[r2], la_lo[r2]], axis=0))
        bl1 = b1[GLA_CHUNK - 1:GLA_CHUNK]
        bl2 = b2[GLA_CHUNK - 1:GLA_CHUNK]
        d1 = jnp.exp(bl1)
        d2 = jnp.exp(bl2)
        qd1 = q[r1] * jnp.exp(b1)
        qd2 = q[r2] * jnp.exp(b2)
        ks1 = k[r1] * jnp.exp(bl1 - b1)
        ks2 = k[r2] * jnp.exp(bl2 - b2)
        qa_ref[r1] = (q[r1] * jnp.exp(b1 - bl1)).astype(BF16)
        qa_ref[r2] = qd2.astype(BF16)
        ka_ref[r1] = ks1.astype(BF16)
        ka_ref[r2] = (k[r2] * jnp.exp(-b2)).astype(BF16)
        qs_ref[r1] = qd1.astype(BF16)
        qs_ref[r2] = (qd2 * d1).astype(BF16)
        ks_ref[r1] = (ks1 * d2).astype(BF16)
        ks_ref[r2] = ks2.astype(BF16)
        pair_decays.append(d1 * d2)

    pi = lax.broadcasted_iota(jnp.int32, (pair, pair), 0)
    pj = lax.broadcasted_iota(jnp.int32, (pair, pair), 1)
    pair_causal = pi >= pj
    for p in range(rows // pair):
        rs = slice(p * pair, (p + 1) * pair)
        for h in range(GLA_HEADS):
            ks_ = slice(h * GLA_HEAD_K, (h + 1) * GLA_HEAD_K)
            attn = jnp.where(pair_causal, _dot_nt(qa_ref[rs, ks_], ka_ref[rs, ks_]), 0.0)
            attn_ref[h, rs] = attn.astype(BF16)

    states = [state_ref[h] for h in range(GLA_HEADS)]
    for p in range(rows // pair):
        rs = slice(p * pair, (p + 1) * pair)
        for h in range(GLA_HEADS):
            ks_ = slice(h * GLA_HEAD_K, (h + 1) * GLA_HEAD_K)
            vs_ = slice(h * GLA_HEAD_V, (h + 1) * GLA_HEAD_V)
            v_h = v_ref[rs, vs_]
            oacc_ref[rs, vs_] = (_dot(attn_ref[h, rs], v_h)
                                 + _dot_nt(qs_ref[rs, ks_], states[h].astype(BF16)))
            states[h] = states[h] * pair_decays[p][:, ks_] + _dot_tn(v_h, ks_ref[rs, ks_])
    for h in range(GLA_HEADS):
        state_ref[h] = states[h]

    gain = on_ref[...]
    y = x
    for h in range(GLA_HEADS):
        vs_ = slice(h * GLA_HEAD_V, (h + 1) * GLA_HEAD_V)
        o_h = _rms_norm(oacc_ref[:, vs_], gain) * gate_ref[:, vs_]
        y = y + _dot(o_h.astype(BF16), wo_ref[vs_, :])
    o_ref[...] = y


GLA_N_QKVR = 2 * GLA_DK + 2 * GLA_DV


def _gla_casts(w_in, w_out):
    return [(w_in, GLA_N_QKVR), _whole(w_out)]


def _gla(x3, norm, w_qkvr, w_in, w_gk_up, b_gk, o_norm, w_out, *, casts=()):
    bsz, s, d = x3.shape
    rows = min(GLA_ROWS, s)
    w_gk = jnp.pad(w_in[:, GLA_N_QKVR:], ((0, 0), (0, LANES - GLA_GATE_RANK))).astype(BF16)
    w_up = jnp.pad(w_gk_up, ((0, LANES - GLA_GATE_RANK), (0, 0))).astype(BF16)
    xspec = pl.BlockSpec((None, rows, d), lambda b, i: (b, i, 0))
    return _sub_block_call(
        functools.partial(_gla_body, rows=rows),
        grid=(bsz, s // rows),
        semantics=("arbitrary", "arbitrary"),
        in_specs=[
            xspec,
            _resident((1, d)),
            _resident((d, GLA_N_QKVR)),
            _resident((d, LANES)),
            _resident((LANES, GLA_DK)),
            _resident((1, GLA_DK)),
            _resident((1, GLA_HEAD_V)),
            _resident((GLA_DV, d)),
        ],
        out_spec=xspec,
        out_shape=jax.ShapeDtypeStruct((bsz, s, d), F32),
        scratch_shapes=[
            pltpu.VMEM((GLA_HEADS, GLA_HEAD_V, GLA_HEAD_K), F32),
            pltpu.VMEM((rows, GLA_DK), BF16),
            pltpu.VMEM((rows, GLA_DK), BF16),
            pltpu.VMEM((rows, GLA_DK), BF16),
            pltpu.VMEM((rows, GLA_DK), BF16),
            pltpu.VMEM((rows, GLA_DV), BF16),
            pltpu.VMEM((GLA_HEADS, rows, 2 * GLA_CHUNK), BF16),
            pltpu.VMEM((rows, GLA_DV), F32),
            pltpu.VMEM((rows, GLA_DV), F32),
        ],
        name="gla",
        args=(x3, norm.reshape(1, d), w_qkvr, w_gk, w_up, b_gk.reshape(1, GLA_DK),
              o_norm.reshape(1, GLA_HEAD_V), w_out),
        casts=casts,
    )


SGU_ROWS = 1024
SGU_COLS = 768


def _gelu(x):
    return 0.5 * x * (1.0 + lax.erf(x * (2.0 ** -0.5)))


def _sgu_body(x_ref, g_ref, wi_ref, lng_ref, lnb_ref, ws_ref, bs_ref, wo_ref, o_ref,
              v_ref, uv_ref, *, rows, half):
    x = x_ref[...]
    xn = _rms_norm(x, g_ref[...]).astype(BF16)
    n_col = half // SGU_COLS

    total = jnp.zeros((rows, 1), F32)
    for c in range(n_col):
        lo = c * SGU_COLS
        v_c = _gelu(_dot(xn, wi_ref[:, half + lo:half + lo + SGU_COLS]))
        v_ref[:, lo:lo + SGU_COLS] = v_c
        total = total + jnp.sum(v_c, axis=-1, keepdims=True)
    mu = total * (1.0 / half)
    sq = jnp.zeros((rows, 1), F32)
    for c in range(n_col):
        lo = c * SGU_COLS
        d_c = v_ref[:, lo:lo + SGU_COLS] - mu
        sq = sq + jnp.sum(d_c * d_c, axis=-1, keepdims=True)
    inv = lax.rsqrt(sq * (1.0 / half) + NORM_EPS)

    ri = lax.broadcasted_iota(jnp.int32, (SGU_CHUNK, SGU_CHUNK), 0)
    ci = lax.broadcasted_iota(jnp.int32, (SGU_CHUNK, SGU_CHUNK), 1)
    causal = ri >= ci
    group_dim = half // SGU_GROUPS
    n_chunks = rows // SGU_CHUNK
    y = x
    for c in range(n_col):
        u_c = _gelu(_dot(xn, wi_ref[:, c * SGU_COLS:(c + 1) * SGU_COLS]))
        for g in range(c * SGU_COLS // group_dim, (c + 1) * SGU_COLS // group_dim):
            lo = g * group_dim
            cols = slice(lo, lo + group_dim)
            vn = ((v_ref[:, cols] - mu) * inv * lng_ref[:, cols] + lnb_ref[:, cols]).astype(BF16)
            w_g = jnp.where(causal, ws_ref[g], 0.0).astype(BF16)
            bias = bs_ref[:, g:g + 1]
            vn_wide = jnp.concatenate(
                [vn[j * SGU_CHUNK:(j + 1) * SGU_CHUNK] for j in range(n_chunks)], axis=1)
            mixed = _dot(w_g, vn_wide) + bias
            for j in range(n_chunks):
                rs = slice(j * SGU_CHUNK, (j + 1) * SGU_CHUNK)
                u_g = u_c[rs, lo - c * SGU_COLS:lo - c * SGU_COLS + group_dim]
                uv_ref[rs, lo - c * SGU_COLS:lo - c * SGU_COLS + group_dim] = (
                    u_g * mixed[:, j * group_dim:(j + 1) * group_dim]).astype(BF16)
        y = y + _dot(uv_ref[...], wo_ref[c * SGU_COLS:(c + 1) * SGU_COLS, :])
    o_ref[...] = y


def _sgu(x2, norm, w_in, ln_gain, ln_bias, w_s, b_s, w_out, *, casts=()):
    t, d = x2.shape
    half = w_out.shape[0]
    rows = min(SGU_ROWS, t)
    return _sub_block_call(
        functools.partial(_sgu_body, rows=rows, half=half),
        grid=(t // rows,),
        semantics=("arbitrary",),
        in_specs=[
            pl.BlockSpec((rows, d), lambda i: (i, 0)),
            _resident((1, d)),
            _resident((d, 2 * half)),
            _resident((1, half)),
            _resident((1, half)),
            _resident((SGU_GROUPS, SGU_CHUNK, SGU_CHUNK)),
            _resident((SGU_CHUNK, SGU_GROUPS)),
            _resident((half, d)),
        ],
        out_spec=pl.BlockSpec((rows, d), lambda i: (i, 0)),
        out_shape=jax.ShapeDtypeStruct((t, d), F32),
        scratch_shapes=[
            pltpu.VMEM((rows, half), F32),
            pltpu.VMEM((rows, SGU_COLS), BF16),
        ],
        name="sgu",
        args=(x2, norm.reshape(1, d), w_in, ln_gain.reshape(1, half), ln_bias.reshape(1, half),
              w_s, b_s.T, w_out),
        casts=casts,
    )


SWA_ROWS = 1024


def _swa_body(x_ref, pos_ref, g_ref, wqkv_ref, bqkv_ref, invf_ref, sinks_ref, wo_ref, bo_ref,
              o_ref, klo_ref, khi_ref, vlo_ref, vhi_ref, q_ref, p_ref, attn_ref, *, rows):
    d_q = SWA_HEADS * SWA_HEAD_DIM
    d_kv = SWA_KV_HEADS * SWA_HEAD_DIM
    n_blocks = rows // SWA_BLOCK
    tiles_per_kv = SWA_GROUP * SWA_HEAD_DIM // LANES
    step = pl.program_id(1)
    kv_refs = (klo_ref, khi_ref, vlo_ref, vhi_ref)

    @pl.when(step == 0)
    def _():
        for ref in kv_refs:
            ref[:, 0:SWA_BLOCK] = jnp.zeros((SWA_KV_HEADS, SWA_BLOCK, ref.shape[-1]), BF16)

    @pl.when(step > 0)
    def _():
        for ref in kv_refs:
            ref[:, 0:SWA_BLOCK] = ref[:, rows:rows + SWA_BLOCK]

    x = x_ref[...]
    xn = _rms_norm(x, g_ref[...]).astype(BF16)

    kv = _dot(xn, wqkv_ref[:, d_q:d_q + 2 * d_kv]) + bqkv_ref[:, d_q:d_q + 2 * d_kv]
    tiles = [kv[:, :d_kv]]
    q_cols = 2 * LANES
    for c in range(d_q // q_cols):
        q2 = _dot(xn, wqkv_ref[:, c * q_cols:(c + 1) * q_cols]) + bqkv_ref[:, c * q_cols:(c + 1) * q_cols]
        tiles += [q2[:, t * LANES:(t + 1) * LANES] for t in range(q_cols // LANES)]

    ang = invf_ref[...] * pos_ref[...].astype(F32)
    sel = (lax.broadcasted_iota(jnp.int32, (3 * ROPE_HALF, LANES), 0) % ROPE_HALF
           == lax.broadcasted_iota(jnp.int32, (3 * ROPE_HALF, LANES), 1) % ROPE_HALF).astype(F32)

    def spread(c):
        hi = c.astype(BF16).astype(F32)
        mid = (c - hi).astype(BF16).astype(F32)
        lo = (c - hi - mid).astype(BF16).astype(F32)
        return _dot_tn(jnp.concatenate([hi, mid, lo], axis=0), sel)

    lane = lax.broadcasted_iota(jnp.int32, (1, LANES), 1)
    head_lane = lane % SWA_HEAD_DIM
    sin_t = spread(jnp.sin(ang))
    cos_t = jnp.where(head_lane < ROPE_DIM, spread(jnp.cos(ang)), 1.0)
    sin_s = jnp.where(head_lane < ROPE_HALF, -sin_t, jnp.where(head_lane < ROPE_DIM, sin_t, 0.0))

    src = lax.broadcasted_iota(jnp.int32, (LANES, LANES), 0)
    dst = lax.broadcasted_iota(jnp.int32, (LANES, LANES), 1)
    dst_head_lane = dst % SWA_HEAD_DIM
    swap = (((dst_head_lane < ROPE_HALF) & (src == dst + ROPE_HALF))
            | ((dst_head_lane >= ROPE_HALF) & (dst_head_lane < ROPE_DIM) & (src == dst - ROPE_HALF)))
    swap2 = jnp.concatenate([swap.astype(BF16), swap.astype(BF16)], axis=0)

    def partner_dot(t):
        hi = t.astype(BF16)
        lo = (t - hi.astype(F32)).astype(BF16)
        return _dot(jnp.concatenate([hi, lo], axis=1), swap2)

    partners = [partner_dot(t) for t in tiles]
    roped = [t * cos_t + p * sin_s for t, p in zip(tiles, partners)]

    low_half = lane < SWA_HEAD_DIM
    cur = slice(SWA_BLOCK, SWA_BLOCK + rows)
    for t, lo_ref, hi_ref in ((roped[0], klo_ref, khi_ref), (kv[:, d_kv:], vlo_ref, vhi_ref)):
        swapped = pltpu.roll(t, SWA_HEAD_DIM, 1)
        lo_ref[0, cur, 0:LANES] = jnp.where(low_half, t, 0.0).astype(BF16)
        hi_ref[0, cur, 0:LANES] = jnp.where(low_half, 0.0, swapped).astype(BF16)
        lo_ref[1, cur, 0:LANES] = jnp.where(low_half, swapped, 0.0).astype(BF16)
        hi_ref[1, cur, 0:LANES] = jnp.where(low_half, 0.0, t).astype(BF16)
    ones_lo = jnp.broadcast_to(jnp.where(low_half, 1.0, 0.0), (rows, LANES)).astype(BF16)
    ones_hi = jnp.broadcast_to(jnp.where(low_half, 0.0, 1.0), (rows, LANES)).astype(BF16)
    for h in range(SWA_KV_HEADS):
        vlo_ref[h, cur, LANES:2 * LANES] = ones_lo
        vhi_ref[h, cur, LANES:2 * LANES] = ones_hi

    for tile, q_rot in enumerate(roped[1:]):
        kvh, slot = divmod(tile, tiles_per_kv)
        q_t = (q_rot * (SWA_HEAD_DIM ** -0.5 * LOG2_E)).astype(BF16)
        for j in range(n_blocks):
            q_ref[kvh, j, slot * SWA_BLOCK:(slot + 1) * SWA_BLOCK] = q_t[j * SWA_BLOCK:(j + 1) * SWA_BLOCK]

    qi = lax.broadcasted_iota(jnp.int32, (SWA_BLOCK, 2 * SWA_BLOCK), 0)
    kj = lax.broadcasted_iota(jnp.int32, (SWA_BLOCK, 2 * SWA_BLOCK), 1)
    delta = qi + SWA_BLOCK - kj
    in_window = (delta >= 0) & (delta < SWA_BLOCK)

    items = [(j, kvh) for j in range(n_blocks) for kvh in range(SWA_KV_HEADS)]

    def score_dots(j, kvh):
        band = slice(j * SWA_BLOCK, (j + 2) * SWA_BLOCK)
        qs = q_ref[kvh, j]
        return [_dot_nt(qs, k_ref[kvh, band]) for k_ref in (klo_ref, khi_ref)]

    s_next = score_dots(*items[0])
    for n, (j, kvh) in enumerate(items):
        rs = slice(j * SWA_BLOCK, (j + 1) * SWA_BLOCK)
        band = slice(j * SWA_BLOCK, (j + 2) * SWA_BLOCK)
        if j == 0:
            valid = in_window & (kj >= jnp.where(step > 0, 0, SWA_BLOCK))
        else:
            valid = in_window
        s_cur = s_next
        if n + 1 < len(items):
            s_next = score_dots(*items[n + 1])
        buf = n % 2
        sink_terms = []
        for e in range(2):
            terms_e = []
            for slot in range(tiles_per_kv):
                ts = slice(slot * SWA_BLOCK, (slot + 1) * SWA_BLOCK)
                sink = sinks_ref[(kvh * tiles_per_kv + slot) * 2 + e] * LOG2_E
                scores = jnp.where(valid, s_cur[e][ts], MASKED_SCORE)
                m = jnp.maximum(jnp.max(scores, axis=-1, keepdims=True), sink)
                p_ref[buf, e, ts] = jnp.exp2(scores - m).astype(BF16)
                terms_e.append(jnp.exp2(sink - m))
            sink_terms.append(terms_e)
        out = _dot(p_ref[buf, 0], vlo_ref[kvh, band]) + _dot(p_ref[buf, 1], vhi_ref[kvh, band])
        for slot in range(tiles_per_kv):
            ts = slice(slot * SWA_BLOCK, (slot + 1) * SWA_BLOCK)
            tile = kvh * tiles_per_kv + slot
            denom = out[ts, LANES:2 * LANES] + jnp.where(low_half, sink_terms[0][slot], sink_terms[1][slot])
            attn_ref[rs, tile * LANES:(tile + 1) * LANES] = (out[ts, 0:LANES] / denom).astype(BF16)
    o_ref[...] = x + _dot(attn_ref[...], wo_ref[...]) + bo_ref[...]


def _swa(x3, positions, norm, w_qkv, b_qkv, sinks, w_out, b_out, *, casts=()):
    bsz, s, d = x3.shape
    rows = min(SWA_ROWS, s)
    d_q = SWA_HEADS * SWA_HEAD_DIM
    d_kv = SWA_KV_HEADS * SWA_HEAD_DIM
    n_qkv = d_q + 2 * d_kv
    invf = (ROPE_THETA ** (-jnp.arange(0, ROPE_DIM, 2, dtype=F32) / ROPE_DIM)).reshape(ROPE_HALF, 1)
    xspec = pl.BlockSpec((None, rows, d), lambda b, i: (b, i, 0))

    def kv_buffer(width):
        return pltpu.VMEM((SWA_KV_HEADS, SWA_BLOCK + rows, width), BF16)

    return _sub_block_call(
        functools.partial(_swa_body, rows=rows),
        grid=(bsz, s // rows),
        semantics=("arbitrary", "arbitrary"),
        in_specs=[
            xspec,
            pl.BlockSpec((None, 1, rows), lambda b, i: (b, 0, i)),
            _resident((1, d)),
            _resident((d, n_qkv)),
            _resident((1, n_qkv)),
            _resident((ROPE_HALF, 1)),
            pl.BlockSpec(memory_space=pltpu.SMEM),
            _resident((d_q, d)),
            _resident((1, d)),
        ],
        out_spec=xspec,
        out_shape=jax.ShapeDtypeStruct((bsz, s, d), F32),
        scratch_shapes=[
            kv_buffer(LANES), kv_buffer(LANES), kv_buffer(2 * LANES), kv_buffer(2 * LANES),
            pltpu.VMEM((SWA_KV_HEADS, rows // SWA_BLOCK, d_q // SWA_KV_HEADS, LANES), BF16),
            pltpu.VMEM((2, 2, d_q // SWA_KV_HEADS, 2 * SWA_BLOCK), BF16),
            pltpu.VMEM((rows, d_q), BF16),
        ],
        name="swa",
        args=(x3, positions.reshape(bsz, 1, s), norm.reshape(1, d), w_qkv, b_qkv.reshape(1, n_qkv),
              invf, sinks, w_out, b_out.reshape(1, d)),
        casts=casts,
    )


def kernel(x, positions, l0_ffn1_norm, l0_ffn1_w_in, l0_ffn1_w_out, l0_mix_norm, l0_gla_w_in, l0_gla_w_gk_up, l0_gla_b_gk, l0_gla_o_norm, l0_gla_w_out, l0_ffn2_norm, l0_ffn2_w_in, l0_ffn2_w_out, l1_ffn1_norm, l1_ffn1_w_in, l1_ffn1_w_out, l1_mix_norm, l1_sgu_w_in, l1_sgu_ln_gain, l1_sgu_ln_bias, l1_sgu_w_s, l1_sgu_b_s, l1_sgu_w_out, l1_ffn2_norm, l1_ffn2_w_in, l1_ffn2_w_out, l2_ffn1_norm, l2_ffn1_w_in, l2_ffn1_w_out, l2_mix_norm, l2_swa_w_qkv, l2_swa_b_qkv, l2_swa_sinks, l2_swa_w_out, l2_swa_b_out, l2_ffn2_norm, l2_ffn2_w_in, l2_ffn2_w_out, l3_ffn1_norm, l3_ffn1_w_in, l3_ffn1_w_out, l3_mix_norm, l3_gla_w_in, l3_gla_w_gk_up, l3_gla_b_gk, l3_gla_o_norm, l3_gla_w_out, l3_ffn2_norm, l3_ffn2_w_in, l3_ffn2_w_out, final_norm):
    bsz, s, d = x.shape

    def ffn(t, norm, w_bf16, casts, final=False):
        y, nxt = _ffn(t.reshape(bsz * s, d), norm, *w_bf16, final_norm, final=final, casts=casts)
        return y.reshape(bsz, s, d), nxt

    def ffn_casts(w_in, w_out):
        return [_whole(w_in), _whole(w_out)]

    w = [l0_ffn1_w_in.astype(BF16), l0_ffn1_w_out.astype(BF16)]
    x, w = ffn(x, l0_ffn1_norm, w, _gla_casts(l0_gla_w_in, l0_gla_w_out))
    x, w = _gla(x, l0_mix_norm, w[0], l0_gla_w_in, l0_gla_w_gk_up, l0_gla_b_gk, l0_gla_o_norm, w[1],
                casts=ffn_casts(l0_ffn2_w_in, l0_ffn2_w_out))
    x, w = ffn(x, l0_ffn2_norm, w, ffn_casts(l1_ffn1_w_in, l1_ffn1_w_out))
    x, w = ffn(x, l1_ffn1_norm, w, ffn_casts(l1_sgu_w_in, l1_sgu_w_out))
    x, w = _sgu(x.reshape(bsz * s, d), l1_mix_norm, w[0], l1_sgu_ln_gain, l1_sgu_ln_bias,
                l1_sgu_w_s, l1_sgu_b_s, w[1], casts=ffn_casts(l1_ffn2_w_in, l1_ffn2_w_out))
    x = x.reshape(bsz, s, d)
    x, w = ffn(x, l1_ffn2_norm, w, ffn_casts(l2_ffn1_w_in, l2_ffn1_w_out))
    x, w = ffn(x, l2_ffn1_norm, w, ffn_casts(l2_swa_w_qkv, l2_swa_w_out))
    x, w = _swa(x, positions, l2_mix_norm, w[0], l2_swa_b_qkv, l2_swa_sinks, w[1], l2_swa_b_out,
                casts=ffn_casts(l2_ffn2_w_in, l2_ffn2_w_out))
    x, w = ffn(x, l2_ffn2_norm, w, ffn_casts(l3_ffn1_w_in, l3_ffn1_w_out))
    x, w = ffn(x, l3_ffn1_norm, w, _gla_casts(l3_gla_w_in, l3_gla_w_out))
    x, w = _gla(x, l3_mix_norm, w[0], l3_gla_w_in, l3_gla_w_gk_up, l3_gla_b_gk, l3_gla_o_norm, w[1],
                casts=ffn_casts(l3_ffn2_w_in, l3_ffn2_w_out))
    x, _ = ffn(x, l3_ffn2_norm, w, [], final=True)
    return x
```
